```python
import math
import jax
import jax.numpy as jnp
from jax import lax
import numpy as np

D_MODEL = 4096
BATCH = 4
SEQ = 2048
DEPTH = 4
DEC_BATCH = 128
DEC_SEQ = 8
PAST_LEN = 8192
PAGE_SIZE = 128

MLA_HEADS = 8
MLA_NOPE = 128
MLA_ROPE = 64
MLA_V = 128
Q_LORA = 768
KV_LORA = 256
SB_HEADS = 16
SB_KV_HEADS = 1
SB_GROUP = SB_HEADS // SB_KV_HEADS
SB_DK = 64
SB_DV = 64
GLA_HEADS = 4
GLA_DK = 128
GLA_DV = 256
GLA_GATE_RANK = 16
GLA_TAU = 16.0
GLA_CHUNK = 32
S5_WIDTH = 1024
S5_GROUP = 16
S5_GROUPS = S5_WIDTH // S5_GROUP
S5_STATE = 64
S5_DT_MIN = 0.001
S5_DT_MAX = 0.1
N_BRANCH = 4
BRANCH_WIDTH = 1024
D_FF = ((8 * D_MODEL + 3 * 256 - 1) // (3 * 256)) * 256
Q_BLOCK = 128
ROPE_THETA = 10000.0
NORM_EPS = 1e-6
NEG_INF = -1e30
MLA_SCALE = 1.0 / math.sqrt(MLA_NOPE + MLA_ROPE)
SB_SCALE = 1.0 / math.sqrt(SB_DK)
GLA_SCALE = 1.0 / math.sqrt(GLA_DK)
IN_SPLITS = (Q_LORA, KV_LORA, MLA_ROPE,
             SB_HEADS * SB_DK, SB_KV_HEADS * SB_DK, SB_KV_HEADS * SB_DV,
             GLA_HEADS * GLA_DK, GLA_HEADS * GLA_DK, GLA_HEADS * GLA_DV, GLA_GATE_RANK, GLA_HEADS * GLA_DV,
             S5_WIDTH)
IN_COLS = sum(IN_SPLITS)

kernel_name = 'hybrid_mla_stickbreak_gla_s5_decode_step'


def rms_norm(x, g):
    xf = x.astype(jnp.float32)
    y = xf * lax.rsqrt(jnp.mean(xf * xf, axis=-1, keepdims=True) + NORM_EPS)
    return (y * g.astype(jnp.float32)).astype(x.dtype)


def rope(x, pos):
    half = x.shape[-1] // 2
    inv = ROPE_THETA ** (-jnp.arange(half, dtype=jnp.float32) / half)
    ang = pos.astype(jnp.float32)[:, None] * inv[None, :]
    shape = (ang.shape[0],) + (1,) * (x.ndim - 3) + (half,)
    cos = jnp.cos(ang).reshape(shape)
    sin = jnp.sin(ang).reshape(shape)
    xf = x.astype(jnp.float32)
    x1, x2 = xf[..., :half], xf[..., half:]
    return jnp.concatenate([x1 * cos - x2 * sin, x1 * sin + x2 * cos], axis=-1).astype(x.dtype)


def split_cols(z):
    out, o = [], 0
    for n in IN_SPLITS:
        out.append(z[..., o:o + n])
        o += n
    return out


def mla_queries(c_q, norm_cq, w_uq, qn_nope, qn_rope, pos):
    B, T, _ = c_q.shape
    q = (rms_norm(c_q, norm_cq) @ w_uq).reshape(B, T, MLA_HEADS, MLA_NOPE + MLA_ROPE)
    q_nope = rms_norm(q[..., :MLA_NOPE], qn_nope)
    q_rope = rope(rms_norm(q[..., MLA_NOPE:], qn_rope), pos)
    return q_nope, q_rope


def mla_latents(c_kv, k_rope_raw, norm_ckv, kn_rope, pos):
    return rms_norm(c_kv, norm_ckv), rope(rms_norm(k_rope_raw, kn_rope), pos)


def mla_expand(latent, w_ukv, kn_nope):
    B, T, _ = latent.shape
    kv = (latent @ w_ukv).reshape(B, T, MLA_HEADS, MLA_NOPE + MLA_V)
    return rms_norm(kv[..., :MLA_NOPE], kn_nope), kv[..., MLA_NOPE:]


def mla_partial(q_nope, q_rope, k_nope, k_rope, v, mask):
    s = (jnp.einsum('bqhd,bkhd->bhqk', q_nope, k_nope).astype(jnp.float32)
         + jnp.einsum('bqhd,bkd->bhqk', q_rope, k_rope).astype(jnp.float32)) * MLA_SCALE
    if mask is not None:
        s = jnp.where(mask, s, NEG_INF)
    m = jnp.max(s, axis=-1)
    p = jnp.exp(s - m[..., None])
    return m, jnp.sum(p, axis=-1), jnp.einsum('bhqk,bkhd->bhqd', p, v.astype(jnp.float32))


def softmax_merge(a, b):
    m1, l1, acc1 = a
    m2, l2, acc2 = b
    m = jnp.maximum(m1, m2)
    e1, e2 = jnp.exp(m1 - m), jnp.exp(m2 - m)
    return m, l1 * e1 + l2 * e2, acc1 * e1[..., None] + acc2 * e2[..., None]


def mla_prompt(q_nope, q_rope, k_nope, k_rope, v):
    B, T, H, _ = q_nope.shape
    nb = T // Q_BLOCK
    qn = q_nope.reshape(B, nb, Q_BLOCK, H, MLA_NOPE).swapaxes(0, 1)
    qr = q_rope.reshape(B, nb, Q_BLOCK, H, MLA_ROPE).swapaxes(0, 1)
    kpos = jnp.arange(T)

    def block(args):
        qn_b, qr_b, i = args
        qpos = i * Q_BLOCK + jnp.arange(Q_BLOCK)
        m, l, acc = mla_partial(qn_b, qr_b, k_nope, k_rope, v, kpos[None, :] <= qpos[:, None])
        return acc / l[..., None]

    out = lax.map(block, (qn, qr, jnp.arange(nb)))
    return out.transpose(1, 0, 3, 2, 4).reshape(B, T, H * MLA_V)


def mla_sample(q_nope, q_rope, lat_new, kr_new, cache_lat, cache_kr, page_table, layer, w_ukv, kn_nope):
    B, T, H, _ = q_nope.shape
    init = (jnp.full((B, H, T), NEG_INF, jnp.float32), jnp.zeros((B, H, T), jnp.float32),
            jnp.zeros((B, H, T, MLA_V), jnp.float32))

    def step(carry, pages):
        k_nope, v = mla_expand(cache_lat[layer, pages], w_ukv, kn_nope)
        part = mla_partial(q_nope, q_rope, k_nope, cache_kr[layer, pages], v, None)
        return softmax_merge(carry, part), None

    carry, _ = lax.scan(step, init, page_table.T)
    k_nope, v = mla_expand(lat_new, w_ukv, kn_nope)
    causal = jnp.tril(jnp.ones((T, T), dtype=bool))
    m, l, acc = softmax_merge(carry, mla_partial(q_nope, q_rope, k_nope, kr_new, v, causal))
    return (acc / l[..., None]).transpose(0, 2, 1, 3).reshape(B, T, H * MLA_V)


def sb_block(q, k, v, mask, suffix):
    z = jnp.einsum('bqhgd,bkhd->bhgqk', q, k).astype(jnp.float32) * SB_SCALE
    lsn = jax.nn.log_sigmoid(-z)
    if mask is not None:
        lsn = jnp.where(mask, lsn, 0.0)
    later = lax.cumsum(lsn, axis=lsn.ndim - 1, reverse=True) - lsn
    w = jnp.exp(jax.nn.log_sigmoid(z) + later + suffix[..., None])
    if mask is not None:
        w = jnp.where(mask, w, 0.0)
    out = jnp.einsum('bhgqk,bkhd->bhgqd', w, v.astype(jnp.float32))
    return out, suffix + jnp.sum(lsn, axis=-1)


def sb_prompt(q, k, v):
    B, T = q.shape[:2]
    nb = T // Q_BLOCK
    qb = q.reshape(B, nb, Q_BLOCK, SB_KV_HEADS, SB_GROUP, SB_DK).swapaxes(0, 1)
    kpos = jnp.arange(T)
    zero = jnp.zeros((B, SB_KV_HEADS, SB_GROUP, Q_BLOCK), jnp.float32)

    def block(args):
        q_b, i = args
        qpos = i * Q_BLOCK + jnp.arange(Q_BLOCK)
        out, _ = sb_block(q_b, k, v, kpos[None, :] < qpos[:, None], zero)
        return out

    out = lax.map(block, (qb, jnp.arange(nb)))
    return out.transpose(1, 0, 4, 2, 3, 5).reshape(B, T, SB_HEADS * SB_DV)


def sb_sample(q, k_new, v_new, cache_k, cache_v, page_table, layer):
    B, T = q.shape[:2]
    strict = jnp.tril(jnp.ones((T, T), dtype=bool), k=-1)
    out, suffix = sb_block(q, k_new, v_new, strict, jnp.zeros((B, SB_KV_HEADS, SB_GROUP, T), jnp.float32))

    def step(carry, pages):
        acc, suf = carry
        o, suf = sb_block(q, cache_k[layer, pages], cache_v[layer, pages], None, suf)
        return (acc + o, suf), None

    (out, _), _ = lax.scan(step, (out, suffix), page_table.T[::-1])
    return out.transpose(0, 3, 1, 2, 4).reshape(B, T, SB_HEADS * SB_DV)


def gla_scan(q, k, v, log_a, s0):
    B, T, H, DK = q.shape
    DV = v.shape[-1]
    C = math.gcd(T, GLA_CHUNK)
    n = T // C

    def chunks(x):
        return x.astype(jnp.float32).reshape(B, n, C, H, x.shape[-1]).transpose(1, 0, 3, 2, 4)

    causal = jnp.tril(jnp.ones((C, C), dtype=bool))

    def step(S, inp):
        qc, kc, vc, gc = inp
        b = jnp.cumsum(gc, axis=2)
        b_last = b[:, :, -1:, :]
        qi = qc * jnp.exp(b)
        att = jnp.where(causal, jnp.einsum('bhcd,bhed->bhce', qi, kc * jnp.exp(-b)), 0.0)
        o = jnp.einsum('bhce,bhef->bhcf', att, vc) + jnp.einsum('bhcd,bhdf->bhcf', qi, S)
        S = jnp.exp(b_last[:, :, 0, :])[..., None] * S + jnp.einsum('bhcd,bhcf->bhdf', kc * jnp.exp(b_last - b), vc)
        return S, o

    S, o = lax.scan(step, s0.astype(jnp.float32), (chunks(q), chunks(k), chunks(v), chunks(log_a)))
    return o.transpose(1, 0, 3, 2, 4).reshape(B, T, H, DV), S


def s5_discretise(a_re, a_im, log_dt, b_re, b_im):
    dt = jnp.exp(log_dt.astype(jnp.float32))[:, None]
    a_re = a_re.astype(jnp.float32)
    a_im = a_im.astype(jnp.float32)
    mag = jnp.exp(a_re * dt)
    ab_re, ab_im = mag * jnp.cos(a_im * dt), mag * jnp.sin(a_im * dt)
    den = a_re * a_re + a_im * a_im
    f_re = ((ab_re - 1.0) * a_re + ab_im * a_im) / den
    f_im = (ab_im * a_re - (ab_re - 1.0) * a_im) / den
    bb_re = f_re[..., None] * b_re - f_im[..., None] * b_im
    bb_im = f_re[..., None] * b_im + f_im[..., None] * b_re
    return ab_re, ab_im, bb_re, bb_im


def s5_combine(e1, e2):
    a1r, a1i, b1r, b1i = e1
    a2r, a2i, b2r, b2i = e2
    return (a2r * a1r - a2i * a1i, a2r * a1i + a2i * a1r,
            a2r * b1r - a2i * b1i + b2r, a2r * b1i + a2i * b1r + b2i)


def s5_layer(u, x0_re, x0_im, prm):
    B, T, _ = u.shape
    ug = u.astype(jnp.float32).reshape(B, T, S5_GROUPS, S5_GROUP)
    ab_re, ab_im, bb_re, bb_im = s5_discretise(prm['s5_a_re'], prm['s5_a_im'], prm['s5_log_dt'],
                                               prm['s5_b_re'], prm['s5_b_im'])
    bu_re = jnp.einsum('btgc,gnc->btgn', ug, bb_re)
    bu_im = jnp.einsum('btgc,gnc->btgn', ug, bb_im)
    x0_re = x0_re.astype(jnp.float32)
    x0_im = x0_im.astype(jnp.float32)
    bu_re = bu_re.at[:, 0].add(ab_re * x0_re - ab_im * x0_im)
    bu_im = bu_im.at[:, 0].add(ab_re * x0_im + ab_im * x0_re)
    elems = (jnp.broadcast_to(ab_re, bu_re.shape), jnp.broadcast_to(ab_im, bu_re.shape), bu_re, bu_im)
    _, _, xs_re, xs_im = lax.associative_scan(s5_combine, elems, axis=1)
    y = (jnp.einsum('gcn,btgn->btgc', prm['s5_c_re'], xs_re)
         - jnp.einsum('gcn,btgn->btgc', prm['s5_c_im'], xs_im)
         + prm['s5_d'].reshape(S5_GROUPS, S5_GROUP) * ug)
    y = jax.nn.gelu(y.reshape(B, T, S5_WIDTH))
    y = y * jax.nn.sigmoid(y @ prm['w_s5_glu'] + prm['b_s5_glu'])
    return y, xs_re[:, -1], xs_im[:, -1]


def merge_and_ffn(h, hn, branches, prm):
    gates = jax.nn.sigmoid(hn @ prm['w_gates'] + prm['b_gates'])
    merged = None
    for i, br in enumerate(branches):
        term = gates[..., i * D_MODEL:(i + 1) * D_MODEL] * (br.astype(h.dtype) @ prm['w_branch'][i])
        merged = term if merged is None else merged + term
    h = h + merged @ prm['w_out']
    hn2 = rms_norm(h, prm['norm_ffn'])
    return h + (jax.nn.silu(hn2 @ prm['w_ff_gate']) * (hn2 @ prm['w_ff_up'])) @ prm['w_ff_down']


def trunk_layer(h, pos, prm, past):
    B, T, _ = h.shape
    hn = rms_norm(h, prm['norm_mix'])
    (c_q, c_kv, kr_raw, sb_q, sb_k, sb_v, g_q, g_k, g_v, g_low, g_r, s5_u) = split_cols(hn @ prm['w_in'])
    q_nope, q_rope = mla_queries(c_q, prm['norm_cq'], prm['w_uq'], prm['qn_nope'], prm['qn_rope'], pos)
    lat, krope = mla_latents(c_kv, kr_raw, prm['norm_ckv'], prm['kn_rope'], pos)
    sb_q = sb_q.reshape(B, T, SB_KV_HEADS, SB_GROUP, SB_DK)
    sb_k = sb_k.reshape(B, T, SB_KV_HEADS, SB_DK)
    sb_v = sb_v.reshape(B, T, SB_KV_HEADS, SB_DV)
    g_q = g_q.reshape(B, T, GLA_HEADS, GLA_DK) * GLA_SCALE
    g_k = g_k.reshape(B, T, GLA_HEADS, GLA_DK)
    g_v = g_v.reshape(B, T, GLA_HEADS, GLA_DV)
    log_a = (jax.nn.log_sigmoid((g_low @ prm['w_gla_gate'] + prm['b_gla_gate']).astype(jnp.float32))
             / GLA_TAU).reshape(B, T, GLA_HEADS, GLA_DK)
    if past is None:
        k_nope, v_a = mla_expand(lat, prm['w_ukv'], prm['kn_nope'])
        out_a = mla_prompt(q_nope, q_rope, k_nope, krope, v_a)
        out_b = sb_prompt(sb_q, sb_k, sb_v)
        gla0 = jnp.zeros((B, GLA_HEADS, GLA_DK, GLA_DV), jnp.float32)
        s5_re0 = jnp.zeros((B, S5_GROUPS, S5_STATE), jnp.float32)
        s5_im0 = jnp.zeros((B, S5_GROUPS, S5_STATE), jnp.float32)
    else:
        cache_lat, cache_kr, cache_k, cache_v, gla0, s5_re0, s5_im0, page_table, layer = past
        out_a = mla_sample(q_nope, q_rope, lat, krope, cache_lat, cache_kr, page_table, layer,
                           prm['w_ukv'], prm['kn_nope'])
        out_b = sb_sample(sb_q, sb_k, sb_v, cache_k, cache_v, page_table, layer)
    o_c, gla_s = gla_scan(g_q, g_k, g_v, log_a, gla0)
    out_c = rms_norm(o_c, prm['gla_norm']).reshape(B, T, GLA_HEADS * GLA_DV) * jax.nn.silu(g_r)
    out_d, s5_re, s5_im = s5_layer(s5_u, s5_re0, s5_im0, prm)
    h = merge_and_ffn(h, hn, (out_a, out_b, out_c, out_d), prm)
    return h, (lat, krope, sb_k, sb_v, gla_s, s5_re, s5_im)


def stack_layers(states, i):
    return jnp.stack([s[i] for s in states], axis=0)


def setup_inputs(seed: int = 0) -> dict:
    key = jax.random.key(seed)
    ks = iter(jax.random.split(key, 64))
    f32 = jnp.float32

    def nrm(shape, scale):
        return jax.random.normal(next(ks), shape, f32) * scale

    def gain(shape):
        return 1.0 + nrm(shape, 0.05)

    L = DEPTH
    n_pages = PAST_LEN // PAGE_SIZE
    n_pool = (DEC_BATCH * n_pages * 5) // 4
    x_prompt = nrm((BATCH, SEQ, D_MODEL), 1.0)
    x_sample = nrm((DEC_BATCH, DEC_SEQ, D_MODEL), 1.0)
    cache_mla_latent = nrm((L, n_pool, PAGE_SIZE, KV_LORA), 1.0)
    cache_mla_krope = nrm((L, n_pool, PAGE_SIZE, MLA_ROPE), 1.0)
    cache_sb_k = nrm((L, n_pool, PAGE_SIZE, SB_KV_HEADS, SB_DK), 1.0)
    cache_sb_v = nrm((L, n_pool, PAGE_SIZE, SB_KV_HEADS, SB_DV), 1.0)
    state_gla = nrm((L, DEC_BATCH, GLA_HEADS, GLA_DK, GLA_DV), 0.1)
    state_s5_re = nrm((L, DEC_BATCH, S5_GROUPS, S5_STATE), 0.1)
    state_s5_im = nrm((L, DEC_BATCH, S5_GROUPS, S5_STATE), 0.1)
    page_table = jax.random.permutation(next(ks), n_pool)[:DEC_BATCH * n_pages].reshape(
        DEC_BATCH, n_pages).astype(jnp.int32)
    return {
        'x_prompt': x_prompt, 'x_sample': x_sample,
        'cache_mla_latent': cache_mla_latent, 'cache_mla_krope': cache_mla_krope,
        'cache_sb_k': cache_sb_k, 'cache_sb_v': cache_sb_v,
        'state_gla': state_gla, 'state_s5_re': state_s5_re, 'state_s5_im': state_s5_im,
        'page_table': page_table,
        'norm_mix': gain((L, D_MODEL)),
        'w_in': nrm((L, D_MODEL, IN_COLS), D_MODEL ** -0.5),
        'norm_cq': gain((L, Q_LORA)),
        'w_uq': nrm((L, Q_LORA, MLA_HEADS * (MLA_NOPE + MLA_ROPE)), Q_LORA ** -0.5),
        'norm_ckv': gain((L, KV_LORA)),
        'w_ukv': nrm((L, KV_LORA, MLA_HEADS * (MLA_NOPE + MLA_V)), KV_LORA ** -0.5),
        'qn_nope': gain((L, MLA_NOPE)), 'qn_rope': gain((L, MLA_ROPE)),
        'kn_nope': gain((L, MLA_NOPE)), 'kn_rope': gain((L, MLA_ROPE)),
        'w_gla_gate': nrm((L, GLA_GATE_RANK, GLA_HEADS * GLA_DK), GLA_GATE_RANK ** -0.5),
        'b_gla_gate': nrm((L, GLA_HEADS * GLA_DK), 0.1),
        'gla_norm': gain((L, GLA_DV)),
        's5_a_re': -0.5 + nrm((L, S5_GROUPS, S5_STATE), 0.01),
        's5_a_im': jnp.broadcast_to(jnp.pi * jnp.arange(S5_STATE, dtype=f32), (L, S5_GROUPS, S5_STATE)),
        's5_log_dt': jax.random.uniform(next(ks), (L, S5_GROUPS), f32, math.log(S5_DT_MIN), math.log(S5_DT_MAX)),
        's5_b_re': nrm((L, S5_GROUPS, S5_STATE, S5_GROUP), (2 * S5_GROUP) ** -0.5),
        's5_b_im': nrm((L, S5_GROUPS, S5_STATE, S5_GROUP), (2 * S5_GROUP) ** -0.5),
        's5_c_re': nrm((L, S5_GROUPS, S5_GROUP, S5_STATE), S5_STATE ** -0.5),
        's5_c_im': nrm((L, S5_GROUPS, S5_GROUP, S5_STATE), S5_STATE ** -0.5),
        's5_d': nrm((L, S5_WIDTH), 0.5),
        'w_s5_glu': nrm((L, S5_WIDTH, S5_WIDTH), S5_WIDTH ** -0.5),
        'b_s5_glu': nrm((L, S5_WIDTH), 0.1),
        'w_gates': nrm((L, D_MODEL, N_BRANCH * D_MODEL), D_MODEL ** -0.5),
        'b_gates': nrm((L, N_BRANCH * D_MODEL), 0.1),
        'w_branch': nrm((L, N_BRANCH, BRANCH_WIDTH, D_MODEL), BRANCH_WIDTH ** -0.5),
        'w_out': nrm((L, D_MODEL, D_MODEL), D_MODEL ** -0.5),
        'norm_ffn': gain((L, D_MODEL)),
        'w_ff_gate': nrm((L, D_MODEL, D_FF), D_MODEL ** -0.5),
        'w_ff_up': nrm((L, D_MODEL, D_FF), D_MODEL ** -0.5),
        'w_ff_down': nrm((L, D_FF, D_MODEL), D_FF ** -0.5),
    }


def reference(x_prompt, x_sample, cache_mla_latent, cache_mla_krope, cache_sb_k, cache_sb_v,
              state_gla, state_s5_re, state_s5_im, page_table,
              norm_mix, w_in, norm_cq, w_uq, norm_ckv, w_ukv, qn_nope, qn_rope, kn_nope, kn_rope,
              w_gla_gate, b_gla_gate, gla_norm,
              s5_a_re, s5_a_im, s5_log_dt, s5_b_re, s5_b_im, s5_c_re, s5_c_im, s5_d, w_s5_glu, b_s5_glu,
              w_gates, b_gates, w_branch, w_out, norm_ffn, w_ff_gate, w_ff_up, w_ff_down):
    past_len = page_table.shape[1] * cache_mla_latent.shape[2]
    pos_p = jnp.arange(x_prompt.shape[1])
    pos_s = past_len + jnp.arange(x_sample.shape[1])
    hp, hs = x_prompt, x_sample
    sp, ss = [], []
    for l in range(DEPTH):
        prm = {
            'norm_mix': norm_mix[l], 'w_in': w_in[l], 'norm_cq': norm_cq[l], 'w_uq': w_uq[l],
            'norm_ckv': norm_ckv[l], 'w_ukv': w_ukv[l], 'qn_nope': qn_nope[l], 'qn_rope': qn_rope[l],
            'kn_nope': kn_nope[l], 'kn_rope': kn_rope[l],
            'w_gla_gate': w_gla_gate[l], 'b_gla_gate': b_gla_gate[l], 'gla_norm': gla_norm[l],
            's5_a_re': s5_a_re[l], 's5_a_im': s5_a_im[l], 's5_log_dt': s5_log_dt[l],
            's5_b_re': s5_b_re[l], 's5_b_im': s5_b_im[l], 's5_c_re': s5_c_re[l], 's5_c_im': s5_c_im[l],
            's5_d': s5_d[l], 'w_s5_glu': w_s5_glu[l], 'b_s5_glu': b_s5_glu[l],
            'w_gates': w_gates[l], 'b_gates': b_gates[l], 'w_branch': w_branch[l], 'w_out': w_out[l],
            'norm_ffn': norm_ffn[l], 'w_ff_gate': w_ff_gate[l], 'w_ff_up': w_ff_up[l], 'w_ff_down': w_ff_down[l],
        }
        hp, st_p = trunk_layer(hp, pos_p, prm, None)
        past = (cache_mla_latent, cache_mla_krope, cache_sb_k, cache_sb_v,
                state_gla[l], state_s5_re[l], state_s5_im[l], page_table, l)
        hs, st_s = trunk_layer(hs, pos_s, prm, past)
        sp.append(st_p)
        ss.append(st_s)
    return (hp, hs,
            stack_layers(sp, 0), stack_layers(sp, 1), stack_layers(sp, 2), stack_layers(sp, 3),
            stack_layers(sp, 4), stack_layers(sp, 5), stack_layers(sp, 6),
            stack_layers(ss, 0), stack_layers(ss, 1), stack_layers(ss, 2), stack_layers(ss, 3),
            stack_layers(ss, 4), stack_layers(ss, 5), stack_layers(ss, 6))
```

```python
import functools
import math

import jax
import jax.numpy as jnp
from jax import lax
from jax.experimental import pallas as pl
from jax.experimental.pallas import tpu as pltpu

D_MODEL = 4096
DEPTH = 4
PAGE_SIZE = 128
MLA_HEADS = 8
MLA_NOPE = 128
MLA_ROPE = 64
MLA_V = 128
Q_LORA = 768
KV_LORA = 256
SB_HEADS = 16
SB_DK = 64
SB_DV = 64
GLA_HEADS = 4
GLA_DK = 128
GLA_DV = 256
GLA_GATE_RANK = 16
GLA_TAU = 16.0
GLA_CHUNK = 32
S5_WIDTH = 1024
S5_GROUP = 16
S5_GROUPS = S5_WIDTH // S5_GROUP
S5_STATE = 64
N_BRANCH = 4
BRANCH_WIDTH = 1024
ROPE_THETA = 10000.0
NORM_EPS = 1e-6
NEG_INF = -1e30
MLA_SCALE = 1.0 / math.sqrt(MLA_NOPE + MLA_ROPE)
SB_SCALE = 1.0 / math.sqrt(SB_DK)
GLA_SCALE = 1.0 / math.sqrt(GLA_DK)

F32 = jnp.float32
BF16 = jnp.bfloat16
LANES = 128
VMEM_LIMIT_MB = 56
MLA_QW = 256
S5_NSTATE = S5_GROUPS * S5_STATE
S5_KT = 4
SCAN_ROWS = 8

C_SBQ = 0
C_GV = 1024
C_GR = 2048
C_S5 = 3072
C_GQ = 4096
C_CQ = 4608
C_CKV = 5376
C_GK = 5632
C_M1 = 6144
C_M2 = 6272
Z_COLS = 6400

NT = (((1,), (1,)), ((), ()))
TN = (((0,), (0,)), ((), ()))


def _cp(sem):
    return pltpu.CompilerParams(dimension_semantics=sem, vmem_limit_bytes=VMEM_LIMIT_MB * 2**20)


def _pick(n, target, mult):
    t = (min(n, target) // mult) * mult
    while t >= mult:
        if n % t == 0:
            return t
        t -= mult
    return n


def _log_sigmoid(x):
    return jnp.minimum(x, 0.0) - jnp.log1p(jnp.exp(-jnp.abs(x)))


def _split_bf16(x):
    hi = x.astype(BF16)
    lo = (x - hi.astype(F32)).astype(BF16)
    return hi, lo


def _rmsnorm_kernel(l_ref, x_ref, g_ref, o_ref):
    x = x_ref[...]
    y = x * lax.rsqrt(jnp.mean(x * x, axis=-1, keepdims=True) + NORM_EPS)
    o_ref[...] = (y * g_ref[...]).astype(o_ref.dtype)


def _rmsnorm(l, x, g):
    M, D = x.shape
    tm = _pick(M, 256, 8)
    return pl.pallas_call(
        _rmsnorm_kernel,
        grid_spec=pltpu.PrefetchScalarGridSpec(
            num_scalar_prefetch=1, grid=(M // tm,),
            in_specs=[pl.BlockSpec((tm, D), lambda i, l: (i, 0)),
                      pl.BlockSpec((None, 1, D), lambda i, l: (l[0], 0, 0))],
            out_specs=pl.BlockSpec((tm, D), lambda i, l: (i, 0))),
        out_shape=jax.ShapeDtypeStruct((M, D), BF16),
        compiler_params=_cp(("parallel",)),
        name="rmsnorm",
    )(l, x, g)


def _mm_kernel(l_ref, a_ref, w_ref, *rest, cm, has_res):
    if has_res:
        r_ref, o_ref = rest
    else:
        (o_ref,) = rest

    def chunk(c, carry):
        rows = pl.ds(pl.multiple_of(c * cm, cm), cm)
        acc = jnp.dot(a_ref[rows, :].astype(BF16), w_ref[...], preferred_element_type=F32)
        if has_res:
            acc = acc + r_ref[rows, :]
        o_ref[rows, :] = acc.astype(o_ref.dtype)
        return carry

    lax.fori_loop(0, a_ref.shape[0] // cm, chunk, 0)


def _matmul(l, a, w, *, res=None, out_dtype=F32, tm_target=1024, tn_target=512):
    M, K = a.shape
    N = w.shape[-1]
    tm = _pick(M, tm_target, 8)
    tn = _pick(N, tn_target, LANES)
    cm = _pick(tm, 256, 8)
    in_specs = [pl.BlockSpec((tm, K), lambda i, j, l: (i, 0)),
                pl.BlockSpec((None, K, tn), lambda i, j, l: (l[0], 0, j))]
    args = [a, w]
    if res is not None:
        in_specs.append(pl.BlockSpec((tm, tn), lambda i, j, l: (i, j)))
        args.append(res)
    return pl.pallas_call(
        functools.partial(_mm_kernel, cm=cm, has_res=res is not None),
        grid_spec=pltpu.PrefetchScalarGridSpec(
            num_scalar_prefetch=1, grid=(M // tm, N // tn),
            in_specs=in_specs,
            out_specs=pl.BlockSpec((tm, tn), lambda i, j, l: (i, j))),
        out_shape=jax.ShapeDtypeStruct((M, N), out_dtype),
        compiler_params=_cp(("parallel", "arbitrary")),
        name="matmul",
    )(l, *args)


def _ffn_gu_kernel(l_ref, a_ref, wg_ref, wu_ref, o_ref, *, cm):
    def chunk(c, carry):
        rows = pl.ds(pl.multiple_of(c * cm, cm), cm)
        a = a_ref[rows, :]
        g = jnp.dot(a, wg_ref[...], preferred_element_type=F32)
        u = jnp.dot(a, wu_ref[...], preferred_element_type=F32)
        o_ref[rows, :] = (g * jax.nn.sigmoid(g) * u).astype(o_ref.dtype)
        return carry

    lax.fori_loop(0, a_ref.shape[0] // cm, chunk, 0)


def _ffn_gate_up(l, a, wg, wu):
    M, K = a.shape
    N = wg.shape[-1]
    tm = _pick(M, 1024, 8)
    tn = _pick(N, 256, LANES)
    cm = _pick(tm, 256, 8)
    return pl.pallas_call(
        functools.partial(_ffn_gu_kernel, cm=cm),
        grid_spec=pltpu.PrefetchScalarGridSpec(
            num_scalar_prefetch=1, grid=(M // tm, N // tn),
            in_specs=[pl.BlockSpec((tm, K), lambda i, j, l: (i, 0)),
                      pl.BlockSpec((None, K, tn), lambda i, j, l: (l[0], 0, j)),
                      pl.BlockSpec((None, K, tn), lambda i, j, l: (l[0], 0, j))],
            out_specs=pl.BlockSpec((tm, tn), lambda i, j, l: (i, j))),
        out_shape=jax.ShapeDtypeStruct((M, N), BF16),
        compiler_params=_cp(("parallel", "arbitrary")),
        name="ffn_gate_up",
    )(l, a, wg, wu)


def _merge_kernel(l_ref, hn_ref, wg_ref, bg_ref, b0_ref, b1_ref, b2_ref, b3_ref, wb_ref, o_ref, acc_ref, *, cm):
    bi = pl.program_id(2)
    n_chunks = hn_ref.shape[0] // cm

    def run(br_ref, first, last):
        def chunk(c, carry):
            rows = pl.ds(pl.multiple_of(c * cm, cm), cm)
            g = jnp.dot(hn_ref[rows, :], wg_ref[...], preferred_element_type=F32) + bg_ref[...]
            t = jnp.dot(br_ref[rows, :].astype(BF16), wb_ref[...], preferred_element_type=F32)
            term = jax.nn.sigmoid(g) * t
            if not first:
                term = acc_ref[rows, :] + term
            if last:
                o_ref[rows, :] = term.astype(o_ref.dtype)
            else:
                acc_ref[rows, :] = term
            return carry

        lax.fori_loop(0, n_chunks, chunk, 0)

    for k, br_ref in enumerate((b0_ref, b1_ref, b2_ref, b3_ref)):
        pl.when(bi == k)(functools.partial(run, br_ref, k == 0, k == N_BRANCH - 1))


def _merge(l, hn, wg, bg, branches, wb):
    M, D = hn.shape
    wide = any(b.dtype == F32 for b in branches)
    tm = _pick(M, 512 if wide else 1024, 8)
    tn = _pick(D, 256, LANES)
    cm = _pick(tm, 256, 8)
    nj = D // tn
    bw = branches[0].shape[1]
    br_spec = pl.BlockSpec((tm, bw), lambda i, j, b, l: (i, 0))
    return pl.pallas_call(
        functools.partial(_merge_kernel, cm=cm),
        grid_spec=pltpu.PrefetchScalarGridSpec(
            num_scalar_prefetch=1, grid=(M // tm, nj, N_BRANCH),
            in_specs=[pl.BlockSpec((tm, D), lambda i, j, b, l: (i, 0)),
                      pl.BlockSpec((None, D, tn), lambda i, j, b, l: (l[0], 0, b * nj + j)),
                      pl.BlockSpec((None, 1, tn), lambda i, j, b, l: (l[0], 0, b * nj + j)),
                      br_spec, br_spec, br_spec, br_spec,
                      pl.BlockSpec((None, None, bw, tn), lambda i, j, b, l: (l[0], b, 0, j))],
            out_specs=pl.BlockSpec((tm, tn), lambda i, j, b, l: (i, j)),
            scratch_shapes=[pltpu.VMEM((tm, tn), F32)]),
        out_shape=jax.ShapeDtypeStruct((M, D), BF16),
        compiler_params=_cp(("parallel", "arbitrary", "arbitrary")),
        name="merge",
    )(l, hn, wg, bg, *branches, wb)


def _rope128(x, cos, sa, sb):
    return x * cos + pltpu.roll(x, LANES - MLA_ROPE // 2, 1) * sa + pltpu.roll(x, MLA_ROPE // 2, 1) * sb


def _mla_prep_kernel(l_ref, cq_ref, ckv_ref, m1_ref, cos_ref, sa_ref, sb_ref,
                     ncq_ref, wuq_ref, nckv_ref, wkv_ref, qnn_ref, qnr_ref, knn_ref, knr_ref,
                     *outs, sample):
    if sample:
        qatt_ref, qabs_ref, lat_ref, kr_ref = outs
    else:
        qatt_ref, lat_ref, kr_ref, katt_ref, vatt_ref = outs
    cos, sa, sb = cos_ref[...], sa_ref[...], sb_ref[...]
    cq = cq_ref[...]
    cqn = cq * lax.rsqrt(jnp.mean(cq * cq, axis=-1, keepdims=True) + NORM_EPS) * ncq_ref[...]
    q = jnp.dot(cqn.astype(BF16), wuq_ref[...], preferred_element_type=F32)
    for h in range(MLA_HEADS):
        qn = q[:, h * MLA_QW:h * MLA_QW + MLA_NOPE]
        qn = qn * lax.rsqrt(jnp.mean(qn * qn, axis=-1, keepdims=True) + NORM_EPS) * qnn_ref[...]
        qr = q[:, h * MLA_QW + MLA_NOPE:(h + 1) * MLA_QW]
        qr = qr * lax.rsqrt(jnp.sum(qr * qr, axis=-1, keepdims=True) / MLA_ROPE + NORM_EPS) * qnr_ref[...]
        qr = _rope128(qr, cos, sa, sb) * MLA_SCALE
        if sample:
            qg = (qn * knn_ref[...] * MLA_SCALE).astype(BF16)
            qatt_ref[:, h * MLA_QW:h * MLA_QW + MLA_NOPE] = qg.astype(F32)
            qabs_ref[:, h * MLA_QW:(h + 1) * MLA_QW] = jnp.dot(
                qg, wkv_ref[h * MLA_NOPE:(h + 1) * MLA_NOPE, :], preferred_element_type=F32)
        else:
            qatt_ref[:, h * MLA_QW:h * MLA_QW + MLA_NOPE] = (qn * MLA_SCALE).astype(qatt_ref.dtype)
        qatt_ref[:, h * MLA_QW + MLA_NOPE:(h + 1) * MLA_QW] = qr.astype(qatt_ref.dtype)

    ckv = ckv_ref[...]
    lat = ckv * lax.rsqrt(jnp.mean(ckv * ckv, axis=-1, keepdims=True) + NORM_EPS) * nckv_ref[...]
    lat_ref[...] = lat
    lane = lax.broadcasted_iota(jnp.int32, m1_ref.shape, 1)
    kr = jnp.where(lane < MLA_ROPE, m1_ref[...], 0.0)
    kr = kr * lax.rsqrt(jnp.sum(kr * kr, axis=-1, keepdims=True) / MLA_ROPE + NORM_EPS) * knr_ref[...]
    kr = _rope128(kr, cos, sa, sb)
    kr_ref[...] = kr
    if not sample:
        kv = jnp.dot(lat.astype(BF16), wkv_ref[...], preferred_element_type=F32)
        w = MLA_NOPE + MLA_V
        for h in range(MLA_HEADS):
            kn = kv[:, h * w:h * w + MLA_NOPE]
            kn = kn * lax.rsqrt(jnp.mean(kn * kn, axis=-1, keepdims=True) + NORM_EPS) * knn_ref[...]
            katt_ref[:, h * MLA_QW:h * MLA_QW + MLA_NOPE] = kn.astype(BF16)
            katt_ref[:, h * MLA_QW + MLA_NOPE:(h + 1) * MLA_QW] = kr.astype(BF16)
            vatt_ref[:, h * MLA_V:(h + 1) * MLA_V] = kv[:, h * w + MLA_NOPE:(h + 1) * w].astype(BF16)


def _mla_prep(l, z, tabs, W, *, sample):
    M = z.shape[0]
    tm = _pick(min(M, tabs[0].shape[0]), 256, 8)
    nt = tabs[0].shape[0] // tm
    HQ = MLA_HEADS * MLA_QW
    row = lambda w, c: pl.BlockSpec((tm, w), lambda i, l: (i, c))
    tab = pl.BlockSpec((tm, LANES), lambda i, l: (i % nt, 0))
    par = lambda a: pl.BlockSpec((None,) + a.shape[1:], lambda i, l: (l[0],) + (0,) * (a.ndim - 1))
    wkv = W['wk_t'] if sample else W['w_ukv']
    params = [W['norm_cq'], W['w_uq'], W['norm_ckv'], wkv, W['qn_nope'], W['qn_rope'], W['kn_nope'], W['kn_rope']]
    if sample:
        out_shape = [jax.ShapeDtypeStruct((M, HQ), F32), jax.ShapeDtypeStruct((M, HQ), F32)]
        out_specs = [row(HQ, 0), row(HQ, 0)]
    else:
        out_shape = [jax.ShapeDtypeStruct((M, HQ), BF16)]
        out_specs = [row(HQ, 0)]
    out_shape += [jax.ShapeDtypeStruct((M, KV_LORA), F32), jax.ShapeDtypeStruct((M, LANES), F32)]
    out_specs += [row(KV_LORA, 0), row(LANES, 0)]
    if not sample:
        out_shape += [jax.ShapeDtypeStruct((M, HQ), BF16), jax.ShapeDtypeStruct((M, MLA_HEADS * MLA_V), BF16)]
        out_specs += [row(HQ, 0), row(MLA_HEADS * MLA_V, 0)]
    return pl.pallas_call(
        functools.partial(_mla_prep_kernel, sample=sample),
        grid_spec=pltpu.PrefetchScalarGridSpec(
            num_scalar_prefetch=1, grid=(M // tm,),
            in_specs=[row(Q_LORA, C_CQ // Q_LORA), row(KV_LORA, C_CKV // KV_LORA), row(LANES, C_M1 // LANES),
                      tab, tab, tab] + [par(p) for p in params],
            out_specs=out_specs),
        out_shape=out_shape,
        compiler_params=_cp(("parallel",)),
        name="mla_prep_sample" if sample else "mla_prep_prompt",
    )(l, z, z, z, *tabs, *params)


def _mla_prompt_kernel(q_ref, k_ref, v_ref, o_ref, m_ref, l_ref, acc_ref, *, tq):
    qi = pl.program_id(2)
    q = q_ref[...]
    m_ref[...] = jnp.full(m_ref.shape, NEG_INF, F32)
    l_ref[...] = jnp.zeros(l_ref.shape, F32)
    acc_ref[...] = jnp.zeros(acc_ref.shape, F32)
    row = qi * tq + lax.broadcasted_iota(jnp.int32, (tq, tq), 0)
    col0 = lax.broadcasted_iota(jnp.int32, (tq, tq), 1)

    def body(j, carry):
        rows = pl.ds(pl.multiple_of(j * tq, tq), tq)
        s = lax.dot_general(q, k_ref[rows, :], NT, preferred_element_type=F32)
        s = jnp.where(col0 + j * tq <= row, s, NEG_INF)
        m_prev = m_ref[...]
        m_new = jnp.maximum(m_prev, jnp.max(s, axis=-1, keepdims=True))
        alpha = jnp.exp(m_prev - m_new)
        p = jnp.exp(s - m_new)
        l_ref[...] = alpha * l_ref[...] + jnp.sum(p, axis=-1, keepdims=True)
        acc_ref[...] = alpha * acc_ref[...] + jnp.dot(p.astype(BF16), v_ref[rows, :], preferred_element_type=F32)
        m_ref[...] = m_new
        return carry

    lax.fori_loop(0, qi + 1, body, 0)
    o_ref[...] = (acc_ref[...] / l_ref[...]).astype(o_ref.dtype)


def _mla_prompt(q_att, k_att, v_att, B, T):
    tq = _pick(T, 512, 16)
    nq = T // tq
    return pl.pallas_call(
        functools.partial(_mla_prompt_kernel, tq=tq),
        grid=(B, MLA_HEADS, nq),
        in_specs=[pl.BlockSpec((tq, MLA_QW), lambda b, h, i: (b * nq + i, h)),
                  pl.BlockSpec((T, MLA_QW), lambda b, h, i: (b, h)),
                  pl.BlockSpec((T, MLA_V), lambda b, h, i: (b, h))],
        out_specs=pl.BlockSpec((tq, MLA_V), lambda b, h, i: (b * nq + i, h)),
        out_shape=jax.ShapeDtypeStruct((B * T, MLA_HEADS * MLA_V), BF16),
        scratch_shapes=[pltpu.VMEM((tq, 1), F32), pltpu.VMEM((tq, 1), F32), pltpu.VMEM((tq, MLA_V), F32)],
        compiler_params=_cp(("parallel", "parallel", "arbitrary")),
        name="mla_prompt",
    )(q_att, k_att, v_att)


def _mla_sample_kernel(pt_ref, l_ref, qabs_ref, qatt_ref, *rest, P, nq):
    lat_refs, kr_refs = rest[:P], rest[P:2 * P]
    latn_ref, krn_ref, wkt_ref, wv_ref, o_ref, qp_s, qr_s, m_s, l_s, acc_s, latb, krb = rest[2 * P:]
    g = pl.program_id(1)
    R = MLA_HEADS * nq

    @pl.when(g == 0)
    def _():
        qp_s[...] = jnp.concatenate(
            [qabs_ref[:, h * MLA_QW:(h + 1) * MLA_QW] for h in range(MLA_HEADS)], axis=0).astype(BF16)
        qr_s[...] = jnp.concatenate(
            [qatt_ref[:, h * MLA_QW + MLA_NOPE:h * MLA_QW + MLA_NOPE + MLA_ROPE] for h in range(MLA_HEADS)],
            axis=0).astype(BF16)
        m_s[...] = jnp.full(m_s.shape, NEG_INF, F32)
        l_s[...] = jnp.zeros(l_s.shape, F32)
        acc_s[...] = jnp.zeros(acc_s.shape, F32)

    def block(lat, kr, mask):
        n = lat.shape[0]
        kn = lax.dot_general(wkt_ref[...], lat, NT, preferred_element_type=F32)
        s1 = lax.dot_general(qp_s[...], lat, NT, preferred_element_type=F32)
        s2 = lax.dot_general(qr_s[...], kr, NT, preferred_element_type=F32)
        rs = []
        for h in range(MLA_HEADS):
            x = kn[h * MLA_NOPE:(h + 1) * MLA_NOPE, :]
            r = lax.rsqrt(jnp.sum(x * x, axis=0, keepdims=True) / MLA_NOPE + NORM_EPS)
            rs.append(jnp.broadcast_to(r, (nq, n)))
        s = s1 * jnp.concatenate(rs, axis=0) + s2
        if mask is not None:
            s = jnp.where(mask, s, NEG_INF)
        m_prev = m_s[...]
        m_new = jnp.maximum(m_prev, jnp.max(s, axis=-1, keepdims=True))
        alpha = jnp.exp(m_prev - m_new)
        p = jnp.exp(s - m_new)
        l_s[...] = alpha * l_s[...] + jnp.sum(p, axis=-1, keepdims=True)
        acc_s[...] = alpha * acc_s[...] + jnp.dot(p.astype(BF16), lat, preferred_element_type=F32)
        m_s[...] = m_new

    for i in range(P):
        latb[i * PAGE_SIZE:(i + 1) * PAGE_SIZE, :] = lat_refs[i][...].astype(BF16)
        krb[i * PAGE_SIZE:(i + 1) * PAGE_SIZE, :] = kr_refs[i][...].astype(BF16)
    block(latb[...], krb[...], None)

    @pl.when(g == pl.num_programs(1) - 1)
    def _():
        pad = PAGE_SIZE - nq
        lat_n = jnp.concatenate([latn_ref[...], jnp.zeros((pad, KV_LORA), F32)], axis=0).astype(BF16)
        kr_n = jnp.concatenate([krn_ref[:, :MLA_ROPE], jnp.zeros((pad, MLA_ROPE), F32)], axis=0).astype(BF16)
        t = lax.broadcasted_iota(jnp.int32, (R, PAGE_SIZE), 0) % nq
        key = lax.broadcasted_iota(jnp.int32, (R, PAGE_SIZE), 1)
        block(lat_n, kr_n, key <= t)
        a = acc_s[...] / l_s[...]
        for h in range(MLA_HEADS):
            hi, lo = _split_bf16(a[h * nq:(h + 1) * nq, :])
            o_ref[:, h * MLA_V:(h + 1) * MLA_V] = (
                jnp.dot(hi, wv_ref[h], preferred_element_type=F32) + jnp.dot(lo, wv_ref[h], preferred_element_type=F32))


def _mla_sample(l, pt, qabs, qatt, cache_lat, cache_kr, lat_new, kr_new, W, Bs, nq):
    NP = pt.shape[0] // Bs
    P = _pick(NP, 8, 1)
    HQ = MLA_HEADS * MLA_QW
    R = MLA_HEADS * nq

    def page(width, i):
        return pl.BlockSpec((None, None, PAGE_SIZE, width),
                            lambda b, g, pt, l: (l[0], pt[b * NP + g * P + i], 0, 0))

    row = lambda w: pl.BlockSpec((nq, w), lambda b, g, pt, l: (b, 0))
    return pl.pallas_call(
        functools.partial(_mla_sample_kernel, P=P, nq=nq),
        grid_spec=pltpu.PrefetchScalarGridSpec(
            num_scalar_prefetch=2, grid=(Bs, NP // P),
            in_specs=[row(HQ), row(HQ)]
            + [page(KV_LORA, i) for i in range(P)] + [page(MLA_ROPE, i) for i in range(P)]
            + [row(KV_LORA), row(LANES),
               pl.BlockSpec((None, MLA_HEADS * MLA_NOPE, KV_LORA), lambda b, g, pt, l: (l[0], 0, 0)),
               pl.BlockSpec((None, MLA_HEADS, KV_LORA, MLA_V), lambda b, g, pt, l: (l[0], 0, 0, 0))],
            out_specs=pl.BlockSpec((nq, MLA_HEADS * MLA_V), lambda b, g, pt, l: (b, 0)),
            scratch_shapes=[pltpu.VMEM((R, KV_LORA), BF16), pltpu.VMEM((R, MLA_ROPE), BF16),
                            pltpu.VMEM((R, 1), F32), pltpu.VMEM((R, 1), F32), pltpu.VMEM((R, KV_LORA), F32),
                            pltpu.VMEM((P * PAGE_SIZE, KV_LORA), BF16), pltpu.VMEM((P * PAGE_SIZE, MLA_ROPE), BF16)]),
        out_shape=jax.ShapeDtypeStruct((Bs * nq, MLA_HEADS * MLA_V), F32),
        compiler_params=_cp(("parallel", "arbitrary")),
        name="mla_sample",
    )(pt, l, qabs, qatt, *([cache_lat] * P), *([cache_kr] * P), lat_new, kr_new, W['wk_t'], W['wv'])


def _sb_head(z, v, later_mat, suffix, mask):
    ls = _log_sigmoid(z)
    lsn = ls - z
    if mask is not None:
        lsn = jnp.where(mask, lsn, 0.0)
    hi, lo = _split_bf16(lsn)
    later = (jnp.dot(hi, later_mat, preferred_element_type=F32)
             + jnp.dot(lo, later_mat, preferred_element_type=F32))
    w = jnp.exp(ls + later + suffix)
    if mask is not None:
        w = jnp.where(mask, w, 0.0)
    o = jnp.dot(w.astype(BF16), v, preferred_element_type=F32)
    return o, suffix + later[:, :1] + lsn[:, :1]


def _later_matrix(n):
    return (lax.broadcasted_iota(jnp.int32, (n, n), 0) > lax.broadcasted_iota(jnp.int32, (n, n), 1)).astype(BF16)


def _sb_prompt_kernel(q_ref, k_ref, v_ref, o_ref, acc0, acc1, *, tq):
    qi = pl.program_id(1)
    lane = lax.broadcasted_iota(jnp.int32, (tq, LANES), 1)
    low = lane < SB_DK
    x = q_ref[...] * SB_SCALE
    q0 = jnp.where(low, x, 0.0).astype(BF16)
    q1 = jnp.where(low, 0.0, x).astype(BF16)
    later_mat = _later_matrix(tq)

    def kv_block(j):
        rows = pl.ds(pl.multiple_of(j * tq, tq), tq)
        kb = k_ref[rows, :]
        vb = v_ref[rows, :]
        kk = jnp.where(low, pltpu.roll(kb, SB_DK, 1), kb).astype(BF16)
        vv = jnp.where(low, vb, pltpu.roll(vb, SB_DV, 1)).astype(BF16)
        return kk, vv

    def heads(kk, vv, s0, s1, mask):
        z0 = lax.dot_general(q0, kk, NT, preferred_element_type=F32)
        z1 = lax.dot_general(q1, kk, NT, preferred_element_type=F32)
        o0, s0 = _sb_head(z0, vv, later_mat, s0, mask)
        o1, s1 = _sb_head(z1, vv, later_mat, s1, mask)
        return o0, o1, s0, s1

    strict = (lax.broadcasted_iota(jnp.int32, (tq, tq), 1) < lax.broadcasted_iota(jnp.int32, (tq, tq), 0))
    zero = jnp.zeros((tq, 1), F32)
    kk, vv = kv_block(qi)
    o0, o1, s0, s1 = heads(kk, vv, zero, zero, strict)
    acc0[...] = o0
    acc1[...] = o1

    def body(it, carry):
        kk, vv = kv_block(qi - 1 - it)
        o0, o1, s0, s1 = heads(kk, vv, carry[0], carry[1], None)
        acc0[...] += o0
        acc1[...] += o1
        return s0, s1

    lax.fori_loop(0, qi, body, (s0, s1))
    o_ref[...] = jnp.where(low, acc0[...], acc1[...]).astype(o_ref.dtype)


def _sb_prompt(z, B, T):
    tq = _pick(T, 256, 16)
    nq = T // tq
    return pl.pallas_call(
        functools.partial(_sb_prompt_kernel, tq=tq),
        grid=(B, nq, SB_HEADS // 2),
        in_specs=[pl.BlockSpec((tq, LANES), lambda b, i, p: (b * nq + i, C_SBQ // LANES + p)),
                  pl.BlockSpec((T, LANES), lambda b, i, p: (b, C_M1 // LANES)),
                  pl.BlockSpec((T, LANES), lambda b, i, p: (b, C_M2 // LANES))],
        out_specs=pl.BlockSpec((tq, LANES), lambda b, i, p: (b * nq + i, p)),
        out_shape=jax.ShapeDtypeStruct((B * T, SB_HEADS * SB_DV), BF16),
        scratch_shapes=[pltpu.VMEM((tq, LANES), F32), pltpu.VMEM((tq, LANES), F32)],
        compiler_params=_cp(("parallel", "parallel", "arbitrary")),
        name="sb_prompt",
    )(z, z, z)


def _sb_sample_kernel(pt_ref, l_ref, q_ref, kn_ref, vn_ref, *rest, P, nq):
    k_refs, v_refs = rest[:P], rest[P:2 * P]
    o_ref, qs, acc, suf = rest[2 * P:]
    g = pl.program_id(1)
    R = SB_HEADS * nq
    later_mat = _later_matrix(PAGE_SIZE)

    @pl.when(g == 0)
    def _():
        x = q_ref[...] * SB_SCALE
        pieces = []
        for h in range(SB_HEADS):
            blk = x[:, (h // 2) * LANES:(h // 2 + 1) * LANES]
            if h % 2:
                blk = pltpu.roll(blk, SB_DK, 1)
            pieces.append(blk[:, :SB_DK])
        qs[...] = jnp.concatenate(pieces, axis=0).astype(BF16)
        pad = PAGE_SIZE - nq
        k_new = pltpu.roll(kn_ref[...], SB_DK, 1)[:, :SB_DK]
        k_new = jnp.concatenate([k_new, jnp.zeros((pad, SB_DK), F32)], axis=0).astype(BF16)
        v_new = jnp.concatenate([vn_ref[:, :SB_DV], jnp.zeros((pad, SB_DV), F32)], axis=0).astype(BF16)
        t = lax.broadcasted_iota(jnp.int32, (R, PAGE_SIZE), 0) % nq
        key = lax.broadcasted_iota(jnp.int32, (R, PAGE_SIZE), 1)
        z = lax.dot_general(qs[...], k_new, NT, preferred_element_type=F32)
        o, s = _sb_head(z, v_new, later_mat, jnp.zeros((R, 1), F32), key < t)
        acc[...] = o
        suf[...] = s

    for i in range(P):
        z = lax.dot_general(qs[...], k_refs[i][...].astype(BF16), NT, preferred_element_type=F32)
        o, s = _sb_head(z, v_refs[i][...].astype(BF16), later_mat, suf[...], None)
        acc[...] += o
        suf[...] = s

    @pl.when(g == pl.num_programs(1) - 1)
    def _():
        o_ref[...] = acc[...]


def _sb_sample(l, pt, z, cache_k, cache_v, Bs, nq):
    NP = pt.shape[0] // Bs
    P = _pick(NP, 8, 1)
    R = SB_HEADS * nq

    def page(width, i):
        return pl.BlockSpec((None, None, PAGE_SIZE, width),
                            lambda b, g, pt, l: (l[0], pt[b * NP + NP - 1 - (g * P + i)], 0, 0))

    out = pl.pallas_call(
        functools.partial(_sb_sample_kernel, P=P, nq=nq),
        grid_spec=pltpu.PrefetchScalarGridSpec(
            num_scalar_prefetch=2, grid=(Bs, NP // P),
            in_specs=[pl.BlockSpec((nq, SB_HEADS * SB_DK), lambda b, g, pt, l: (b, C_SBQ // (SB_HEADS * SB_DK))),
                      pl.BlockSpec((nq, LANES), lambda b, g, pt, l: (b, C_M1 // LANES)),
                      pl.BlockSpec((nq, LANES), lambda b, g, pt, l: (b, C_M2 // LANES))]
            + [page(SB_DK, i) for i in range(P)] + [page(SB_DV, i) for i in range(P)],
            out_specs=pl.BlockSpec((None, R, SB_DV), lambda b, g, pt, l: (b, 0, 0)),
            scratch_shapes=[pltpu.VMEM((R, SB_DK), BF16), pltpu.VMEM((R, SB_DV), F32), pltpu.VMEM((R, 1), F32)]),
        out_shape=jax.ShapeDtypeStruct((Bs, R, SB_DV), F32),
        compiler_params=_cp(("parallel", "arbitrary")),
        name="sb_sample",
    )(pt, l, z, z, z, *([cache_k] * P), *([cache_v] * P))
    return out.reshape(Bs, SB_HEADS, nq, SB_DV).transpose(0, 2, 1, 3).reshape(Bs * nq, SB_HEADS * SB_DV)


def _gla_kernel(l_ref, q_ref, k_ref, v_ref, gl_ref, gr_ref, wg_ref, bg_ref, gn_ref, *rest, C, has_state):
    if has_state:
        s0_ref, o_ref, sout_ref, S = rest
        S[...] = s0_ref[...]
    else:
        o_ref, sout_ref, S = rest
        S[...] = jnp.zeros(S.shape, F32)
    tril = lax.broadcasted_iota(jnp.int32, (C, C), 0) >= lax.broadcasted_iota(jnp.int32, (C, C), 1)
    csum = tril.astype(BF16)
    ones = jnp.ones((C, GLA_DK), BF16)

    def chunk(c, carry):
        rows = pl.ds(pl.multiple_of(c * C, C), C)
        qc = q_ref[rows, :] * GLA_SCALE
        kc = k_ref[rows, :]
        vc = v_ref[rows, :].astype(BF16)
        pre = jnp.dot(gl_ref[rows, :].astype(BF16), wg_ref[...], preferred_element_type=F32) + bg_ref[...]
        hi, lo = _split_bf16(_log_sigmoid(pre) / GLA_TAU)
        b = jnp.dot(csum, hi, preferred_element_type=F32) + jnp.dot(csum, lo, preferred_element_type=F32)
        b_tot = (lax.dot_general(hi, ones, TN, preferred_element_type=F32)
                 + lax.dot_general(lo, ones, TN, preferred_element_type=F32))
        b_last = b[C - 1:C, :]
        qi = (qc * jnp.exp(b)).astype(BF16)
        att = lax.dot_general(qi, (kc * jnp.exp(-b)).astype(BF16), NT, preferred_element_type=F32)
        att = jnp.where(tril, att, 0.0)
        s_old = S[...]
        o = (jnp.dot(att.astype(BF16), vc, preferred_element_type=F32)
             + jnp.dot(qi, s_old.astype(BF16), preferred_element_type=F32))
        kd = (kc * jnp.exp(b_last - b)).astype(BF16)
        decay = jnp.exp(b_tot)
        decay = jnp.concatenate([decay] * (GLA_DV // GLA_DK), axis=1)
        S[...] = decay * s_old + lax.dot_general(kd, vc, TN, preferred_element_type=F32)
        on = o * lax.rsqrt(jnp.mean(o * o, axis=-1, keepdims=True) + NORM_EPS) * gn_ref[...]
        gr = gr_ref[rows, :]
        o_ref[rows, :] = (on * (gr * jax.nn.sigmoid(gr))).astype(o_ref.dtype)
        return carry

    lax.fori_loop(0, q_ref.shape[0] // C, chunk, 0)
    sout_ref[...] = S[...]


def _gla(l, z, W, B, T, state):
    C = math.gcd(T, GLA_CHUNK)
    has_state = state is not None
    cb = lambda w, c0: pl.BlockSpec((T, w), lambda b, h, l: (b, c0 // w + h))
    in_specs = [cb(GLA_DK, C_GQ), cb(GLA_DK, C_GK), cb(GLA_DV, C_GV),
                pl.BlockSpec((T, LANES), lambda b, h, l: (b, C_M2 // LANES)),
                cb(GLA_DV, C_GR),
                pl.BlockSpec((None, LANES, GLA_DK), lambda b, h, l: (l[0], 0, h)),
                pl.BlockSpec((None, 1, GLA_DK), lambda b, h, l: (l[0], 0, h)),
                pl.BlockSpec((None, 1, GLA_DV), lambda b, h, l: (l[0], 0, 0))]
    args = [z, z, z, z, z, W['w_gla_gate'], W['b_gla_gate'], W['gla_norm']]
    if has_state:
        in_specs.append(pl.BlockSpec((None, None, None, GLA_DK, GLA_DV), lambda b, h, l: (l[0], b, h, 0, 0)))
        args.append(state)
    return pl.pallas_call(
        functools.partial(_gla_kernel, C=C, has_state=has_state),
        grid_spec=pltpu.PrefetchScalarGridSpec(
            num_scalar_prefetch=1, grid=(B, GLA_HEADS),
            in_specs=in_specs,
            out_specs=[pl.BlockSpec((T, GLA_DV), lambda b, h, l: (b, h)),
                       pl.BlockSpec((None, None, GLA_DK, GLA_DV), lambda b, h, l: (b, h, 0, 0))],
            scratch_shapes=[pltpu.VMEM((GLA_DK, GLA_DV), F32)]),
        out_shape=[jax.ShapeDtypeStruct((B * T, GLA_HEADS * GLA_DV), F32 if has_state else BF16),
                   jax.ShapeDtypeStruct((B, GLA_HEADS, GLA_DK, GLA_DV), F32)],
        compiler_params=_cp(("parallel", "parallel")),
        name="gla_sample" if has_state else "gla_prompt",
    )(l, *args)


def _s5_kernel(l_ref, u_ref, bre_ref, bim_ref, pw_ref, cre_ref, cim_ref, d_ref, wglu_ref, bglu_ref, *rest,
               tt, chained):
    if chained:
        o_ref, sre_ref, sim_ref, xre, xim, car_re, car_im = rest
    else:
        x0re_ref, x0im_ref, o_ref, sre_ref, sim_ref, xre, xim = rest
    slab_u = S5_WIDTH // S5_KT
    slab_x = S5_NSTATE // S5_KT
    u = u_ref[...]
    ub = u.astype(BF16)
    for kt in range(S5_KT):
        us = ub[:, kt * slab_u:(kt + 1) * slab_u]
        xre[:, kt * slab_x:(kt + 1) * slab_x] = jnp.dot(us, bre_ref[kt], preferred_element_type=F32)
        xim[:, kt * slab_x:(kt + 1) * slab_x] = jnp.dot(us, bim_ref[kt], preferred_element_type=F32)

    if chained:
        @pl.when(pl.program_id(1) == 0)
        def _():
            car_re[...] = jnp.zeros(car_re.shape, F32)
            car_im[...] = jnp.zeros(car_im.shape, F32)

    def tile(j, carry):
        rows = pl.ds(pl.multiple_of(j * SCAN_ROWS, SCAN_ROWS), SCAN_ROWS)
        for kt in range(S5_KT):
            cols = slice(kt * slab_x, (kt + 1) * slab_x)
            ar, ai = xre[rows, cols], xim[rows, cols]
            for n, k in enumerate((1, 2, 4)):
                pr, pi = pw_ref[2 * n, :, cols], pw_ref[2 * n + 1, :, cols]
                sr, si = pltpu.roll(ar, k, 0), pltpu.roll(ai, k, 0)
                ar, ai = ar + pr * sr - pi * si, ai + pr * si + pi * sr
            if chained:
                cr, ci = car_re[:, cols], car_im[:, cols]
            else:
                cr = jnp.broadcast_to(x0re_ref[pl.ds(j, 1), cols], (SCAN_ROWS, slab_x))
                ci = jnp.broadcast_to(x0im_ref[pl.ds(j, 1), cols], (SCAN_ROWS, slab_x))
            pr, pi = pw_ref[6, :, cols], pw_ref[7, :, cols]
            ar, ai = ar + pr * cr - pi * ci, ai + pr * ci + pi * cr
            xre[rows, cols] = ar
            xim[rows, cols] = ai
            last_r, last_i = ar[SCAN_ROWS - 1:SCAN_ROWS, :], ai[SCAN_ROWS - 1:SCAN_ROWS, :]
            if chained:
                car_re[:, cols] = jnp.broadcast_to(last_r, (SCAN_ROWS, slab_x))
                car_im[:, cols] = jnp.broadcast_to(last_i, (SCAN_ROWS, slab_x))
            else:
                sre_ref[pl.ds(j, 1), cols] = last_r
                sim_ref[pl.ds(j, 1), cols] = last_i
        return carry

    lax.fori_loop(0, tt // SCAN_ROWS, tile, 0)
    if chained:
        sre_ref[...] = car_re[0:1, :]
        sim_ref[...] = car_im[0:1, :]

    ys = []
    for kt in range(S5_KT):
        cols = slice(kt * slab_x, (kt + 1) * slab_x)
        ys.append(jnp.dot(xre[:, cols].astype(BF16), cre_ref[kt], preferred_element_type=F32)
                  - jnp.dot(xim[:, cols].astype(BF16), cim_ref[kt], preferred_element_type=F32))
    y = jnp.concatenate(ys, axis=1) + d_ref[...] * u
    y = jax.nn.gelu(y)
    gate = jax.nn.sigmoid(jnp.dot(y.astype(BF16), wglu_ref[...], preferred_element_type=F32) + bglu_ref[...])
    o_ref[...] = (y * gate).astype(o_ref.dtype)


def _s5(l, z, W, B, T, state):
    chained = state is None
    M = B * T
    par = lambda a: pl.BlockSpec((None,) + a.shape[1:], lambda *g: (g[-1][0],) + (0,) * (a.ndim - 1))
    params = [W['s5_bre'], W['s5_bim'], W['s5_pw'], W['s5_cre'], W['s5_cim'], W['s5_d'], W['w_s5_glu'], W['b_s5_glu']]
    if chained:
        tt = _pick(T, 256, SCAN_ROWS)
        nt = T // tt
        grid = (B, nt)
        u_spec = pl.BlockSpec((tt, S5_WIDTH), lambda b, t, l: (b * nt + t, C_S5 // S5_WIDTH))
        o_spec = pl.BlockSpec((tt, S5_WIDTH), lambda b, t, l: (b * nt + t, 0))
        st_spec = pl.BlockSpec((None, 1, S5_NSTATE), lambda b, t, l: (b, 0, 0))
        st_shape = jax.ShapeDtypeStruct((B, 1, S5_NSTATE), F32)
        in_specs = [u_spec] + [par(p) for p in params]
        args = [z] + params
        scratch = [pltpu.VMEM((tt, S5_NSTATE), F32), pltpu.VMEM((tt, S5_NSTATE), F32),
                   pltpu.VMEM((SCAN_ROWS, S5_NSTATE), F32), pltpu.VMEM((SCAN_ROWS, S5_NSTATE), F32)]
        sem = ("parallel", "arbitrary")
    else:
        assert T == SCAN_ROWS
        tt = _pick(M, 256, 8 * SCAN_ROWS)
        grid = (M // tt,)
        nb = tt // SCAN_ROWS
        u_spec = pl.BlockSpec((tt, S5_WIDTH), lambda i, l: (i, C_S5 // S5_WIDTH))
        o_spec = pl.BlockSpec((tt, S5_WIDTH), lambda i, l: (i, 0))
        st_spec = pl.BlockSpec((nb, S5_NSTATE), lambda i, l: (i, 0))
        st_shape = jax.ShapeDtypeStruct((B, S5_NSTATE), F32)
        x0_spec = pl.BlockSpec((None, nb, S5_NSTATE), lambda i, l: (l[0], i, 0))
        in_specs = [u_spec] + [par(p) for p in params] + [x0_spec, x0_spec]
        args = [z] + params + list(state)
        scratch = [pltpu.VMEM((tt, S5_NSTATE), F32), pltpu.VMEM((tt, S5_NSTATE), F32)]
        sem = ("parallel",)
    return pl.pallas_call(
        functools.partial(_s5_kernel, tt=tt, chained=chained),
        grid_spec=pltpu.PrefetchScalarGridSpec(
            num_scalar_prefetch=1, grid=grid, in_specs=in_specs,
            out_specs=[o_spec, st_spec, st_spec], scratch_shapes=scratch),
        out_shape=[jax.ShapeDtypeStruct((M, S5_WIDTH), BF16 if chained else F32), st_shape, st_shape],
        compiler_params=_cp(sem),
        name="s5_prompt" if chained else "s5_sample",
    )(l, *args)


def _rope_tables(pos):
    half = MLA_ROPE // 2
    inv = ROPE_THETA ** (-jnp.arange(half, dtype=F32) / half)
    ang = pos.astype(F32)[:, None] * inv[None, :]
    cos, sin = jnp.cos(ang), jnp.sin(ang)
    zero = jnp.zeros_like(cos)
    pad = jnp.zeros((pos.shape[0], LANES - MLA_ROPE), F32)
    return (jnp.concatenate([cos, cos, pad], axis=1),
            jnp.concatenate([-sin, zero, pad], axis=1),
            jnp.concatenate([zero, sin, pad], axis=1))


def _s5_tables(a_re, a_im, log_dt, b_re, b_im):
    L = a_re.shape[0]
    dt = jnp.exp(log_dt)[..., None]
    mag = jnp.exp(a_re * dt)
    ab_re, ab_im = mag * jnp.cos(a_im * dt), mag * jnp.sin(a_im * dt)
    den = a_re * a_re + a_im * a_im
    f_re = ((ab_re - 1.0) * a_re + ab_im * a_im) / den
    f_im = (ab_im * a_re - (ab_re - 1.0) * a_im) / den
    bb_re = f_re[..., None] * b_re - f_im[..., None] * b_im
    bb_im = f_re[..., None] * b_im + f_im[..., None] * b_re
    ar, ai = ab_re.reshape(L, S5_NSTATE), ab_im.reshape(L, S5_NSTATE)
    pows = [(ar, ai)]
    for _ in range(SCAN_ROWS - 1):
        pr, pi = pows[-1]
        pows.append((pr * ar - pi * ai, pr * ai + pi * ar))
    idx = jnp.arange(SCAN_ROWS)[None, :, None]
    tabs = []
    for k in (1, 2, 4):
        pr, pi = pows[k - 1]
        tabs.append(jnp.where(idx >= k, pr[:, None, :], 0.0))
        tabs.append(jnp.where(idx >= k, pi[:, None, :], 0.0))
    tabs.append(jnp.stack([p[0] for p in pows], axis=1))
    tabs.append(jnp.stack([p[1] for p in pows], axis=1))
    return bb_re, bb_im, jnp.stack(tabs, axis=1)


def _block_diag_in(bb):
    L = bb.shape[0]
    gpk = S5_GROUPS // S5_KT
    x = bb.reshape(L, S5_KT, gpk, S5_STATE, S5_GROUP)
    eye = jnp.eye(gpk, dtype=bb.dtype)
    y = jnp.einsum('lkgnc,gh->lkgchn', x, eye)
    return y.reshape(L, S5_KT, gpk * S5_GROUP, gpk * S5_STATE).astype(BF16)


def _block_diag_out(c):
    L = c.shape[0]
    gpk = S5_GROUPS // S5_KT
    x = c.reshape(L, S5_KT, gpk, S5_GROUP, S5_STATE)
    eye = jnp.eye(gpk, dtype=c.dtype)
    y = jnp.einsum('lkgcn,gh->lkgnhc', x, eye)
    return y.reshape(L, S5_KT, gpk * S5_STATE, gpk * S5_GROUP).astype(BF16)


def _prepare_weights(p):
    L = p['w_in'].shape[0]
    D = p['w_in'].shape[1]
    W = {}
    offs, o = [], 0
    for n in (Q_LORA, KV_LORA, MLA_ROPE, SB_HEADS * SB_DK, SB_DK, SB_DV, GLA_HEADS * GLA_DK, GLA_HEADS * GLA_DK,
              GLA_HEADS * GLA_DV, GLA_GATE_RANK, GLA_HEADS * GLA_DV, S5_WIDTH):
        offs.append((o, o + n))
        o += n
    (cq, ckv, kr, sbq, sbk, sbv, gq, gk, gv, glow, gr, s5u) = [p['w_in'][:, :, a:b] for a, b in offs]
    zpad = jnp.zeros((L, D, LANES - SB_DV - GLA_GATE_RANK), F32)
    W['w_in'] = jnp.concatenate([sbq, gv, gr, s5u, gq, cq, ckv, gk, kr, sbk, sbv, glow, zpad], axis=2).astype(BF16)
    wq = p['w_uq'].reshape(L, Q_LORA, MLA_HEADS, MLA_NOPE + MLA_ROPE)
    wq = jnp.concatenate([wq, jnp.zeros((L, Q_LORA, MLA_HEADS, MLA_QW - MLA_NOPE - MLA_ROPE), F32)], axis=3)
    W['w_uq'] = wq.reshape(L, Q_LORA, MLA_HEADS * MLA_QW).astype(BF16)
    wkv = p['w_ukv'].astype(BF16)
    W['w_ukv'] = wkv
    wkv4 = wkv.reshape(L, KV_LORA, MLA_HEADS, MLA_NOPE + MLA_V)
    W['wk_t'] = wkv4[..., :MLA_NOPE].transpose(0, 2, 3, 1).reshape(L, MLA_HEADS * MLA_NOPE, KV_LORA)
    W['wv'] = wkv4[..., MLA_NOPE:].transpose(0, 2, 1, 3)
    row = lambda a: a.reshape(L, 1, a.shape[-1])
    padr = lambda a: jnp.concatenate([a, jnp.zeros((L, LANES - MLA_ROPE), F32)], axis=1)
    for k in ('norm_mix', 'norm_cq', 'norm_ckv', 'qn_nope', 'kn_nope', 'b_gla_gate', 'gla_norm', 's5_d', 'b_s5_glu',
              'b_gates', 'norm_ffn'):
        W[k] = row(p[k])
    W['qn_rope'] = row(padr(p['qn_rope']))
    W['kn_rope'] = row(padr(p['kn_rope']))
    wg = jnp.zeros((L, LANES, GLA_HEADS * GLA_DK), F32)
    W['w_gla_gate'] = wg.at[:, SB_DV:SB_DV + GLA_GATE_RANK, :].set(p['w_gla_gate']).astype(BF16)
    bb_re, bb_im, W['s5_pw'] = _s5_tables(p['s5_a_re'], p['s5_a_im'], p['s5_log_dt'], p['s5_b_re'], p['s5_b_im'])
    W['s5_bre'], W['s5_bim'] = _block_diag_in(bb_re), _block_diag_in(bb_im)
    W['s5_cre'], W['s5_cim'] = _block_diag_out(p['s5_c_re']), _block_diag_out(p['s5_c_im'])
    for k in ('w_s5_glu', 'w_gates', 'w_branch', 'w_out', 'w_ff_gate', 'w_ff_up', 'w_ff_down'):
        W[k] = p[k].astype(BF16)
    return W


def _trunk_layer(l, h, W, tabs, B, T, past):
    sample = past is not None
    hn = _rmsnorm(l, h, W['norm_mix'])
    z = _matmul(l, hn, W['w_in'], tn_target=640)
    if sample:
        pt, cache_lat, cache_kr, cache_k, cache_v, st_gla, st_re, st_im = past
        q_att, q_abs, lat, kr = _mla_prep(l, z, tabs, W, sample=True)
        out_a = _mla_sample(l, pt, q_abs, q_att, cache_lat, cache_kr, lat, kr, W, B, T)
        out_b = _sb_sample(l, pt, z, cache_k, cache_v, B, T)
        out_c, gla_s = _gla(l, z, W, B, T, st_gla)
        out_d, s5_re, s5_im = _s5(l, z, W, B, T, (st_re, st_im))
    else:
        q_att, lat, kr, k_att, v_att = _mla_prep(l, z, tabs, W, sample=False)
        out_a = _mla_prompt(q_att, k_att, v_att, B, T)
        out_b = _sb_prompt(z, B, T)
        out_c, gla_s = _gla(l, z, W, B, T, None)
        out_d, s5_re, s5_im = _s5(l, z, W, B, T, None)
    merged = _merge(l, hn, W['w_gates'], W['b_gates'], (out_a, out_b, out_c, out_d), W['w_branch'])
    h = _matmul(l, merged, W['w_out'], res=h)
    hn2 = _rmsnorm(l, h, W['norm_ffn'])
    act = _ffn_gate_up(l, hn2, W['w_ff_gate'], W['w_ff_up'])
    h = _matmul(l, act, W['w_ff_down'], res=h, tm_target=512, tn_target=256)
    states = (lat.reshape(B, T, KV_LORA),
              kr[:, :MLA_ROPE].reshape(B, T, MLA_ROPE),
              z[:, C_M1 + MLA_ROPE:C_M1 + MLA_ROPE + SB_DK].reshape(B, T, 1, SB_DK),
              z[:, C_M2:C_M2 + SB_DV].reshape(B, T, 1, SB_DV),
              gla_s,
              s5_re.reshape(B, S5_GROUPS, S5_STATE),
              s5_im.reshape(B, S5_GROUPS, S5_STATE))
    return h, states


def kernel(x_prompt, x_sample, cache_mla_latent, cache_mla_krope, cache_sb_k, cache_sb_v, state_gla, state_s5_re, state_s5_im, page_table, norm_mix, w_in, norm_cq, w_uq, norm_ckv, w_ukv, qn_nope, qn_rope, kn_nope, kn_rope, w_gla_gate, b_gla_gate, gla_norm, s5_a_re, s5_a_im, s5_log_dt, s5_b_re, s5_b_im, s5_c_re, s5_c_im, s5_d, w_s5_glu, b_s5_glu, w_gates, b_gates, w_branch, w_out, norm_ffn, w_ff_gate, w_ff_up, w_ff_down):
    L = w_in.shape[0]
    B, T, D = x_prompt.shape
    Bs, Ts, _ = x_sample.shape
    n_pool, page = cache_mla_latent.shape[1], cache_mla_latent.shape[2]
    past_len = page_table.shape[1] * page
    W = _prepare_weights(dict(
        norm_mix=norm_mix, w_in=w_in, norm_cq=norm_cq, w_uq=w_uq, norm_ckv=norm_ckv, w_ukv=w_ukv,
        qn_nope=qn_nope, qn_rope=qn_rope, kn_nope=kn_nope, kn_rope=kn_rope,
        w_gla_gate=w_gla_gate, b_gla_gate=b_gla_gate, gla_norm=gla_norm,
        s5_a_re=s5_a_re, s5_a_im=s5_a_im, s5_log_dt=s5_log_dt, s5_b_re=s5_b_re, s5_b_im=s5_b_im,
        s5_c_re=s5_c_re, s5_c_im=s5_c_im, s5_d=s5_d, w_s5_glu=w_s5_glu, b_s5_glu=b_s5_glu,
        w_gates=w_gates, b_gates=b_gates, w_branch=w_branch, w_out=w_out, norm_ffn=norm_ffn,
        w_ff_gate=w_ff_gate, w_ff_up=w_ff_up, w_ff_down=w_ff_down))
    tabs_p = _rope_tables(jnp.arange(T))
    reps = _pick(Bs * Ts, 256, 8) // Ts
    tabs_s = tuple(jnp.tile(t, (reps, 1)) for t in _rope_tables(past_len + jnp.arange(Ts)))
    pt = page_table.reshape(-1).astype(jnp.int32)
    cache_k = cache_sb_k.reshape(L, n_pool, page, SB_DK)
    cache_v = cache_sb_v.reshape(L, n_pool, page, SB_DV)
    st_re = state_s5_re.reshape(L, Bs, S5_NSTATE)
    st_im = state_s5_im.reshape(L, Bs, S5_NSTATE)
    past = (pt, cache_mla_latent, cache_mla_krope, cache_k, cache_v, state_gla, st_re, st_im)

    def layer(carry, li):
        hp, hs = carry
        l = li.reshape(1)
        hp, st_p = _trunk_layer(l, hp, W, tabs_p, B, T, None)
        hs, st_s = _trunk_layer(l, hs, W, tabs_s, Bs, Ts, past)
        return (hp, hs), (st_p, st_s)

    (hp, hs), (sp, ss) = lax.scan(layer, (x_prompt.reshape(B * T, D), x_sample.reshape(Bs * Ts, D)),
                                  jnp.arange(L, dtype=jnp.int32))
    return (hp.reshape(B, T, D), hs.reshape(Bs, Ts, D)) + tuple(sp) + tuple(ss)
```

```python
import functools
import math

import jax
import jax.numpy as jnp
from jax import lax
from jax.experimental import pallas as pl
from jax.experimental.pallas import tpu as pltpu

D_MODEL = 4096
DEPTH = 4
PAGE_SIZE = 128
MLA_HEADS = 8
MLA_NOPE = 128
MLA_ROPE = 64
MLA_V = 128
Q_LORA = 768
KV_LORA = 256
SB_HEADS = 16
SB_DK = 64
SB_DV = 64
GLA_HEADS = 4
GLA_DK = 128
GLA_DV = 256
GLA_GATE_RANK = 16
GLA_TAU = 16.0
GLA_CHUNK = 32
S5_WIDTH = 1024
S5_GROUP = 16
S5_GROUPS = S5_WIDTH // S5_GROUP
S5_STATE = 64
N_BRANCH = 4
BRANCH_WIDTH = 1024
ROPE_THETA = 10000.0
NORM_EPS = 1e-6
NEG_INF = -1e30
MLA_SCALE = 1.0 / math.sqrt(MLA_NOPE + MLA_ROPE)
SB_SCALE = 1.0 / math.sqrt(SB_DK)
GLA_SCALE = 1.0 / math.sqrt(GLA_DK)

F32 = jnp.float32
BF16 = jnp.bfloat16
LANES = 128
VMEM_LIMIT_MB = 56
MLA_QW = 256
S5_NSTATE = S5_GROUPS * S5_STATE
S5_KT = 4
SCAN_ROWS = 8
ROW_CHUNK = 1024

C_SBQ = 0
C_GV = 1024
C_GR = 2048
C_S5 = 3072
C_GQ = 4096
C_CQ = 4608
C_CKV = 5376
C_GK = 5632
C_M1 = 6144
C_M2 = 6272
Z_COLS = 6400

NT = (((1,), (1,)), ((), ()))
TN = (((0,), (0,)), ((), ()))


def _cp(sem):
    return pltpu.CompilerParams(dimension_semantics=sem, vmem_limit_bytes=VMEM_LIMIT_MB * 2**20)


def _pick(n, target, mult):
    t = (min(n, target) // mult) * mult
    while t >= mult:
        if n % t == 0:
            return t
        t -= mult
    return n


def _log_sigmoid(x):
    return jnp.minimum(x, 0.0) - jnp.log(1.0 + jnp.exp(-jnp.abs(x)))


def _split_bf16(x):
    hi = x.astype(BF16)
    lo = (x - hi.astype(F32)).astype(BF16)
    return hi, lo


def _rmsnorm_kernel(l_ref, x_ref, g_ref, o_ref):
    x = x_ref[...]
    y = x * lax.rsqrt(jnp.mean(x * x, axis=-1, keepdims=True) + NORM_EPS)
    o_ref[...] = (y * g_ref[...]).astype(o_ref.dtype)


def _rmsnorm(l, x, g):
    M, D = x.shape
    tm = _pick(M, 256, 8)
    return pl.pallas_call(
        _rmsnorm_kernel,
        grid_spec=pltpu.PrefetchScalarGridSpec(
            num_scalar_prefetch=1, grid=(M // tm,),
            in_specs=[pl.BlockSpec((tm, D), lambda i, l: (i, 0)),
                      pl.BlockSpec((None, 1, D), lambda i, l: (l[0], 0, 0))],
            out_specs=pl.BlockSpec((tm, D), lambda i, l: (i, 0))),
        out_shape=jax.ShapeDtypeStruct((M, D), BF16),
        compiler_params=_cp(("parallel",)),
        name="rmsnorm",
    )(l, x, g)


def _mm_kernel(l_ref, a_ref, w_ref, *rest, cm, has_res):
    if has_res:
        r_ref, o_ref = rest
    else:
        (o_ref,) = rest

    def chunk(c, carry):
        rows = pl.ds(pl.multiple_of(c * cm, cm), cm)
        acc = jnp.dot(a_ref[rows, :].astype(BF16), w_ref[...], preferred_element_type=F32)
        if has_res:
            acc = acc + r_ref[rows, :]
        o_ref[rows, :] = acc.astype(o_ref.dtype)
        return carry

    lax.fori_loop(0, a_ref.shape[0] // cm, chunk, 0)


def _matmul(l, a, w, *, res=None, out_dtype=F32, tm_target=1024, tn_target=512):
    M, K = a.shape
    N = w.shape[-1]
    tm = _pick(M, tm_target, 8)
    tn = _pick(N, tn_target, LANES)
    cm = _pick(tm, ROW_CHUNK, 8)
    in_specs = [pl.BlockSpec((tm, K), lambda i, j, l: (i, 0)),
                pl.BlockSpec((None, K, tn), lambda i, j, l: (l[0], 0, j))]
    args = [a, w]
    if res is not None:
        in_specs.append(pl.BlockSpec((tm, tn), lambda i, j, l: (i, j)))
        args.append(res)
    return pl.pallas_call(
        functools.partial(_mm_kernel, cm=cm, has_res=res is not None),
        grid_spec=pltpu.PrefetchScalarGridSpec(
            num_scalar_prefetch=1, grid=(M // tm, N // tn),
            in_specs=in_specs,
            out_specs=pl.BlockSpec((tm, tn), lambda i, j, l: (i, j))),
        out_shape=jax.ShapeDtypeStruct((M, N), out_dtype),
        compiler_params=_cp(("parallel", "arbitrary")),
        name="matmul",
    )(l, *args)


def _ffn_gu_kernel(l_ref, a_ref, wg_ref, wu_ref, o_ref, *, cm):
    def chunk(c, carry):
        rows = pl.ds(pl.multiple_of(c * cm, cm), cm)
        a = a_ref[rows, :]
        g = jnp.dot(a, wg_ref[...], preferred_element_type=F32)
        u = jnp.dot(a, wu_ref[...], preferred_element_type=F32)
        o_ref[rows, :] = (g * jax.nn.sigmoid(g) * u).astype(o_ref.dtype)
        return carry

    lax.fori_loop(0, a_ref.shape[0] // cm, chunk, 0)


def _ffn_gate_up(l, a, wg, wu):
    M, K = a.shape
    N = wg.shape[-1]
    tm = _pick(M, 1024, 8)
    tn = _pick(N, 256, LANES)
    cm = _pick(tm, ROW_CHUNK, 8)
    return pl.pallas_call(
        functools.partial(_ffn_gu_kernel, cm=cm),
        grid_spec=pltpu.PrefetchScalarGridSpec(
            num_scalar_prefetch=1, grid=(M // tm, N // tn),
            in_specs=[pl.BlockSpec((tm, K), lambda i, j, l: (i, 0)),
                      pl.BlockSpec((None, K, tn), lambda i, j, l: (l[0], 0, j)),
                      pl.BlockSpec((None, K, tn), lambda i, j, l: (l[0], 0, j))],
            out_specs=pl.BlockSpec((tm, tn), lambda i, j, l: (i, j))),
        out_shape=jax.ShapeDtypeStruct((M, N), BF16),
        compiler_params=_cp(("parallel", "arbitrary")),
        name="ffn_gate_up",
    )(l, a, wg, wu)


def _merge_kernel(l_ref, hn_ref, wg_ref, bg_ref, b0_ref, b1_ref, b2_ref, b3_ref, wb_ref, o_ref, acc_ref, *, cm):
    bi = pl.program_id(2)
    n_chunks = hn_ref.shape[0] // cm

    def run(br_ref, first, last):
        def chunk(c, carry):
            rows = pl.ds(pl.multiple_of(c * cm, cm), cm)
            g = jnp.dot(hn_ref[rows, :], wg_ref[...], preferred_element_type=F32) + bg_ref[...]
            t = jnp.dot(br_ref[rows, :].astype(BF16), wb_ref[...], preferred_element_type=F32)
            term = jax.nn.sigmoid(g) * t
            if not first:
                term = acc_ref[rows, :] + term
            if last:
                o_ref[rows, :] = term.astype(o_ref.dtype)
            else:
                acc_ref[rows, :] = term
            return carry

        lax.fori_loop(0, n_chunks, chunk, 0)

    for k, br_ref in enumerate((b0_ref, b1_ref, b2_ref, b3_ref)):
        pl.when(bi == k)(functools.partial(run, br_ref, k == 0, k == N_BRANCH - 1))


def _merge(l, hn, wg, bg, branches, wb):
    M, D = hn.shape
    wide = any(b.dtype == F32 for b in branches)
    tm = _pick(M, 512 if wide else 1024, 8)
    tn = _pick(D, 256, LANES)
    cm = _pick(tm, ROW_CHUNK, 8)
    nj = D // tn
    bw = branches[0].shape[1]
    br_spec = pl.BlockSpec((tm, bw), lambda i, j, b, l: (i, 0))
    return pl.pallas_call(
        functools.partial(_merge_kernel, cm=cm),
        grid_spec=pltpu.PrefetchScalarGridSpec(
            num_scalar_prefetch=1, grid=(M // tm, nj, N_BRANCH),
            in_specs=[pl.BlockSpec((tm, D), lambda i, j, b, l: (i, 0)),
                      pl.BlockSpec((None, D, tn), lambda i, j, b, l: (l[0], 0, b * nj + j)),
                      pl.BlockSpec((None, 1, tn), lambda i, j, b, l: (l[0], 0, b * nj + j)),
                      br_spec, br_spec, br_spec, br_spec,
                      pl.BlockSpec((None, None, bw, tn), lambda i, j, b, l: (l[0], b, 0, j))],
            out_specs=pl.BlockSpec((tm, tn), lambda i, j, b, l: (i, j)),
            scratch_shapes=[pltpu.VMEM((tm, tn), F32)]),
        out_shape=jax.ShapeDtypeStruct((M, D), BF16),
        compiler_params=_cp(("parallel", "arbitrary", "arbitrary")),
        name="merge",
    )(l, hn, wg, bg, *branches, wb)


def _rope128(x, cos, sa, sb):
    return x * cos + pltpu.roll(x, LANES - MLA_ROPE // 2, 1) * sa + pltpu.roll(x, MLA_ROPE // 2, 1) * sb


def _mla_prep_kernel(l_ref, cq_ref, ckv_ref, m1_ref, cos_ref, sa_ref, sb_ref,
                     ncq_ref, wuq_ref, nckv_ref, wkv_ref, qnn_ref, qnr_ref, knn_ref, knr_ref,
                     *outs, sample):
    if sample:
        qatt_ref, qabs_ref, lat_ref, kr_ref = outs
    else:
        qatt_ref, lat_ref, kr_ref, katt_ref, vatt_ref = outs
    cos, sa, sb = cos_ref[...], sa_ref[...], sb_ref[...]
    cq = cq_ref[...]
    cqn = cq * lax.rsqrt(jnp.mean(cq * cq, axis=-1, keepdims=True) + NORM_EPS) * ncq_ref[...]
    q = jnp.dot(cqn.astype(BF16), wuq_ref[...], preferred_element_type=F32)
    for h in range(MLA_HEADS):
        qn = q[:, h * MLA_QW:h * MLA_QW + MLA_NOPE]
        qn = qn * lax.rsqrt(jnp.mean(qn * qn, axis=-1, keepdims=True) + NORM_EPS) * qnn_ref[...]
        qr = q[:, h * MLA_QW + MLA_NOPE:(h + 1) * MLA_QW]
        qr = qr * lax.rsqrt(jnp.sum(qr * qr, axis=-1, keepdims=True) / MLA_ROPE + NORM_EPS) * qnr_ref[...]
        qr = _rope128(qr, cos, sa, sb) * MLA_SCALE
        if sample:
            qg = (qn * knn_ref[...] * MLA_SCALE).astype(BF16)
            qatt_ref[:, h * MLA_QW:h * MLA_QW + MLA_NOPE] = qg.astype(F32)
            qabs_ref[:, h * MLA_QW:(h + 1) * MLA_QW] = jnp.dot(
                qg, wkv_ref[h * MLA_NOPE:(h + 1) * MLA_NOPE, :], preferred_element_type=F32)
        else:
            qatt_ref[:, h * MLA_QW:h * MLA_QW + MLA_NOPE] = (qn * MLA_SCALE).astype(qatt_ref.dtype)
        qatt_ref[:, h * MLA_QW + MLA_NOPE:(h + 1) * MLA_QW] = qr.astype(qatt_ref.dtype)

    ckv = ckv_ref[...]
    lat = ckv * lax.rsqrt(jnp.mean(ckv * ckv, axis=-1, keepdims=True) + NORM_EPS) * nckv_ref[...]
    lat_ref[...] = lat
    lane = lax.broadcasted_iota(jnp.int32, m1_ref.shape, 1)
    kr = jnp.where(lane < MLA_ROPE, m1_ref[...], 0.0)
    kr = kr * lax.rsqrt(jnp.sum(kr * kr, axis=-1, keepdims=True) / MLA_ROPE + NORM_EPS) * knr_ref[...]
    kr = _rope128(kr, cos, sa, sb)
    kr_ref[...] = kr
    if not sample:
        kv = jnp.dot(lat.astype(BF16), wkv_ref[...], preferred_element_type=F32)
        w = MLA_NOPE + MLA_V
        for h in range(MLA_HEADS):
            kn = kv[:, h * w:h * w + MLA_NOPE]
            kn = kn * lax.rsqrt(jnp.mean(kn * kn, axis=-1, keepdims=True) + NORM_EPS) * knn_ref[...]
            katt_ref[:, h * MLA_QW:h * MLA_QW + MLA_NOPE] = kn.astype(BF16)
            katt_ref[:, h * MLA_QW + MLA_NOPE:(h + 1) * MLA_QW] = kr.astype(BF16)
            vatt_ref[:, h * MLA_V:(h + 1) * MLA_V] = kv[:, h * w + MLA_NOPE:(h + 1) * w].astype(BF16)


def _mla_prep(l, z, tabs, W, *, sample):
    M = z.shape[0]
    tm = _pick(min(M, tabs[0].shape[0]), 256, 8)
    nt = tabs[0].shape[0] // tm
    HQ = MLA_HEADS * MLA_QW
    row = lambda w, c: pl.BlockSpec((tm, w), lambda i, l: (i, c))
    tab = pl.BlockSpec((tm, LANES), lambda i, l: (i % nt, 0))
    par = lambda a: pl.BlockSpec((None,) + a.shape[1:], lambda i, l: (l[0],) + (0,) * (a.ndim - 1))
    wkv = W['wk_t'] if sample else W['w_ukv']
    params = [W['norm_cq'], W['w_uq'], W['norm_ckv'], wkv, W['qn_nope'], W['qn_rope'], W['kn_nope'], W['kn_rope']]
    if sample:
        out_shape = [jax.ShapeDtypeStruct((M, HQ), F32), jax.ShapeDtypeStruct((M, HQ), F32)]
        out_specs = [row(HQ, 0), row(HQ, 0)]
    else:
        out_shape = [jax.ShapeDtypeStruct((M, HQ), BF16)]
        out_specs = [row(HQ, 0)]
    out_shape += [jax.ShapeDtypeStruct((M, KV_LORA), F32), jax.ShapeDtypeStruct((M, LANES), F32)]
    out_specs += [row(KV_LORA, 0), row(LANES, 0)]
    if not sample:
        out_shape += [jax.ShapeDtypeStruct((M, HQ), BF16), jax.ShapeDtypeStruct((M, MLA_HEADS * MLA_V), BF16)]
        out_specs += [row(HQ, 0), row(MLA_HEADS * MLA_V, 0)]
    return pl.pallas_call(
        functools.partial(_mla_prep_kernel, sample=sample),
        grid_spec=pltpu.PrefetchScalarGridSpec(
            num_scalar_prefetch=1, grid=(M // tm,),
            in_specs=[row(Q_LORA, C_CQ // Q_LORA), row(KV_LORA, C_CKV // KV_LORA), row(LANES, C_M1 // LANES),
                      tab, tab, tab] + [par(p) for p in params],
            out_specs=out_specs),
        out_shape=out_shape,
        compiler_params=_cp(("parallel",)),
        name="mla_prep_sample" if sample else "mla_prep_prompt",
    )(l, z, z, z, *tabs, *params)


def _mla_prompt_kernel(q_ref, k_ref, v_ref, o_ref, m_ref, l_ref, acc_ref, *, tq, nh):
    qi = pl.program_id(2)
    m_ref[...] = jnp.full(m_ref.shape, NEG_INF, F32)
    l_ref[...] = jnp.zeros(l_ref.shape, F32)
    acc_ref[...] = jnp.zeros(acc_ref.shape, F32)
    row = qi * tq + lax.broadcasted_iota(jnp.int32, (tq, tq), 0)
    col0 = lax.broadcasted_iota(jnp.int32, (tq, tq), 1)
    hs = range(nh)

    def body(j, carry):
        rows = pl.ds(pl.multiple_of(j * tq, tq), tq)
        keep = col0 + j * tq <= row
        s = [lax.dot_general(q_ref[:, h * MLA_QW:(h + 1) * MLA_QW], k_ref[rows, h * MLA_QW:(h + 1) * MLA_QW], NT,
                             preferred_element_type=F32) for h in hs]
        s = [jnp.where(keep, x, NEG_INF) for x in s]
        m_prev = [m_ref[h] for h in hs]
        m_new = [jnp.maximum(m_prev[h], jnp.max(s[h], axis=-1, keepdims=True)) for h in hs]
        p = [jnp.exp(s[h] - m_new[h]) for h in hs]
        pv = [jnp.dot(p[h].astype(BF16), v_ref[rows, h * MLA_V:(h + 1) * MLA_V], preferred_element_type=F32)
              for h in hs]
        for h in hs:
            alpha = jnp.exp(m_prev[h] - m_new[h])
            l_ref[h] = alpha * l_ref[h] + jnp.sum(p[h], axis=-1, keepdims=True)
            acc_ref[h] = alpha * acc_ref[h] + pv[h]
            m_ref[h] = m_new[h]
        return carry

    lax.fori_loop(0, qi + 1, body, 0)
    for h in hs:
        o_ref[:, h * MLA_V:(h + 1) * MLA_V] = (acc_ref[h] / l_ref[h]).astype(o_ref.dtype)


def _mla_prompt(q_att, k_att, v_att, B, T):
    tq = _pick(T, 256, 16)
    nq = T // tq
    nh = 2
    return pl.pallas_call(
        functools.partial(_mla_prompt_kernel, tq=tq, nh=nh),
        grid=(B, MLA_HEADS // nh, nq),
        in_specs=[pl.BlockSpec((tq, nh * MLA_QW), lambda b, h, i: (b * nq + i, h)),
                  pl.BlockSpec((T, nh * MLA_QW), lambda b, h, i: (b, h)),
                  pl.BlockSpec((T, nh * MLA_V), lambda b, h, i: (b, h))],
        out_specs=pl.BlockSpec((tq, nh * MLA_V), lambda b, h, i: (b * nq + i, h)),
        out_shape=jax.ShapeDtypeStruct((B * T, MLA_HEADS * MLA_V), BF16),
        scratch_shapes=[pltpu.VMEM((nh, tq, 1), F32), pltpu.VMEM((nh, tq, 1), F32), pltpu.VMEM((nh, tq, MLA_V), F32)],
        compiler_params=_cp(("parallel", "parallel", "arbitrary")),
        name="mla_prompt",
    )(q_att, k_att, v_att)


def _mla_sample_kernel(pt_ref, l_ref, qabs_ref, qatt_ref, *rest, P, nq):
    lat_refs, kr_refs = rest[:P], rest[P:2 * P]
    latn_ref, krn_ref, wkt_ref, wv_ref, o_ref, qp_s, qr_s, m_s, l_s, acc_s, latb, krb = rest[2 * P:]
    g = pl.program_id(1)
    R = MLA_HEADS * nq

    HN = MLA_HEADS * MLA_NOPE

    @pl.when(g == 0)
    def _():
        qp_s[0:HN, :] = wkt_ref[...]
        qp_s[HN:HN + R, :] = jnp.concatenate(
            [qabs_ref[:, h * MLA_QW:(h + 1) * MLA_QW] for h in range(MLA_HEADS)], axis=0).astype(BF16)
        qr_s[...] = jnp.concatenate(
            [qatt_ref[:, h * MLA_QW + MLA_NOPE:h * MLA_QW + MLA_NOPE + MLA_ROPE] for h in range(MLA_HEADS)],
            axis=0).astype(BF16)
        m_s[...] = jnp.full(m_s.shape, NEG_INF, F32)
        l_s[...] = jnp.zeros(l_s.shape, F32)
        acc_s[...] = jnp.zeros(acc_s.shape, F32)

    def scores(big, s2):
        n = big.shape[1]
        rs = []
        for h in range(MLA_HEADS):
            x = big[h * MLA_NOPE:(h + 1) * MLA_NOPE, :]
            r = lax.rsqrt(jnp.sum(x * x, axis=0, keepdims=True) / MLA_NOPE + NORM_EPS)
            rs.append(jnp.broadcast_to(r, (nq, n)))
        return big[HN:HN + R, :] * jnp.concatenate(rs, axis=0) + s2

    def block(lat, kr, mask):
        n = lat.shape[0]
        sub = min(n, 4 * PAGE_SIZE)
        nsub = n // sub
        expand = lambda j: lax.dot_general(qp_s[...], lat[j * sub:(j + 1) * sub, :], NT, preferred_element_type=F32)
        s2 = lax.dot_general(qr_s[...], kr, NT, preferred_element_type=F32)
        bigs = [expand(0)]
        parts = []
        for j in range(nsub):
            if j + 1 < nsub:
                bigs.append(expand(j + 1))
            parts.append(scores(bigs[j], s2[:, j * sub:(j + 1) * sub]))
        s = parts[0] if nsub == 1 else jnp.concatenate(parts, axis=1)
        if mask is not None:
            s = jnp.where(mask, s, NEG_INF)
        m_prev = m_s[...]
        m_new = jnp.maximum(m_prev, jnp.max(s, axis=-1, keepdims=True))
        alpha = jnp.exp(m_prev - m_new)
        p = jnp.exp(s - m_new)
        l_s[...] = alpha * l_s[...] + jnp.sum(p, axis=-1, keepdims=True)
        acc_s[...] = alpha * acc_s[...] + jnp.dot(p.astype(BF16), lat, preferred_element_type=F32)
        m_s[...] = m_new

    for i in range(P):
        latb[i * PAGE_SIZE:(i + 1) * PAGE_SIZE, :] = lat_refs[i][...].astype(BF16)
        krb[i * PAGE_SIZE:(i + 1) * PAGE_SIZE, :] = kr_refs[i][...].astype(BF16)
    block(latb[...], krb[...], None)

    @pl.when(g == pl.num_programs(1) - 1)
    def _():
        pad = PAGE_SIZE - nq
        lat_n = jnp.concatenate([latn_ref[...], jnp.zeros((pad, KV_LORA), F32)], axis=0).astype(BF16)
        kr_n = jnp.concatenate([krn_ref[:, :MLA_ROPE], jnp.zeros((pad, MLA_ROPE), F32)], axis=0).astype(BF16)
        t = lax.broadcasted_iota(jnp.int32, (R, PAGE_SIZE), 0) % nq
        key = lax.broadcasted_iota(jnp.int32, (R, PAGE_SIZE), 1)
        block(lat_n, kr_n, key <= t)
        a = acc_s[...] / l_s[...]
        for h in range(MLA_HEADS):
            hi, lo = _split_bf16(a[h * nq:(h + 1) * nq, :])
            o_ref[:, h * MLA_V:(h + 1) * MLA_V] = (
                jnp.dot(hi, wv_ref[h], preferred_element_type=F32) + jnp.dot(lo, wv_ref[h], preferred_element_type=F32))


def _mla_sample(l, pt, qabs, qatt, cache_lat, cache_kr, lat_new, kr_new, W, Bs, nq):
    NP = pt.shape[0] // Bs
    P = _pick(NP, 16, 1)
    HQ = MLA_HEADS * MLA_QW
    R = MLA_HEADS * nq

    def page(width, i):
        return pl.BlockSpec((None, None, PAGE_SIZE, width),
                            lambda b, g, pt, l: (l[0], pt[b * NP + g * P + i], 0, 0))

    row = lambda w: pl.BlockSpec((nq, w), lambda b, g, pt, l: (b, 0))
    return pl.pallas_call(
        functools.partial(_mla_sample_kernel, P=P, nq=nq),
        grid_spec=pltpu.PrefetchScalarGridSpec(
            num_scalar_prefetch=2, grid=(Bs, NP // P),
            in_specs=[row(HQ), row(HQ)]
            + [page(KV_LORA, i) for i in range(P)] + [page(MLA_ROPE, i) for i in range(P)]
            + [row(KV_LORA), row(LANES),
               pl.BlockSpec((None, MLA_HEADS * MLA_NOPE, KV_LORA), lambda b, g, pt, l: (l[0], 0, 0)),
               pl.BlockSpec((None, MLA_HEADS, KV_LORA, MLA_V), lambda b, g, pt, l: (l[0], 0, 0, 0))],
            out_specs=pl.BlockSpec((nq, MLA_HEADS * MLA_V), lambda b, g, pt, l: (b, 0)),
            scratch_shapes=[pltpu.VMEM((MLA_HEADS * MLA_NOPE + R, KV_LORA), BF16), pltpu.VMEM((R, MLA_ROPE), BF16),
                            pltpu.VMEM((R, 1), F32), pltpu.VMEM((R, 1), F32), pltpu.VMEM((R, KV_LORA), F32),
                            pltpu.VMEM((P * PAGE_SIZE, KV_LORA), BF16), pltpu.VMEM((P * PAGE_SIZE, MLA_ROPE), BF16)]),
        out_shape=jax.ShapeDtypeStruct((Bs * nq, MLA_HEADS * MLA_V), F32),
        compiler_params=_cp(("parallel", "arbitrary")),
        name="mla_sample",
    )(pt, l, qabs, qatt, *([cache_lat] * P), *([cache_kr] * P), lat_new, kr_new, W['wk_t'], W['wv'])


def _sb_head(z, v, later_mat, suffix, mask):
    ls = _log_sigmoid(z)
    lsn = ls - z
    if mask is not None:
        lsn = jnp.where(mask, lsn, 0.0)
    hi, lo = _split_bf16(lsn)
    later = (jnp.dot(hi, later_mat, preferred_element_type=F32)
             + jnp.dot(lo, later_mat, preferred_element_type=F32))
    w = jnp.exp(ls + later + suffix)
    if mask is not None:
        w = jnp.where(mask, w, 0.0)
    o = jnp.dot(w.astype(BF16), v, preferred_element_type=F32)
    return o, suffix + later[:, :1] + lsn[:, :1]


def _later_matrix(n):
    return (lax.broadcasted_iota(jnp.int32, (n, n), 0) > lax.broadcasted_iota(jnp.int32, (n, n), 1)).astype(BF16)


def _sb_prompt_kernel(q_ref, k_ref, v_ref, o_ref, qs, acc, *, tq, nh):
    qi = pl.program_id(1)
    low = lax.broadcasted_iota(jnp.int32, (tq, LANES), 1) < SB_DK
    for p in range(nh // 2):
        x = q_ref[:, p * LANES:(p + 1) * LANES] * SB_SCALE
        qs[(2 * p) * tq:(2 * p + 1) * tq, :] = jnp.where(low, x, 0.0).astype(BF16)
        qs[(2 * p + 1) * tq:(2 * p + 2) * tq, :] = jnp.where(low, 0.0, x).astype(BF16)
    later_mat = _later_matrix(tq)

    def kv_block(j):
        rows = pl.ds(pl.multiple_of(j * tq, tq), tq)
        kb = k_ref[rows, :]
        vb = v_ref[rows, :]
        kk = jnp.where(low, pltpu.roll(kb, SB_DK, 1), kb).astype(BF16)
        vv = jnp.where(low, vb, pltpu.roll(vb, SB_DV, 1)).astype(BF16)
        return kk, vv

    def step(j, suffix, mask):
        kk, vv = kv_block(j)
        z = lax.dot_general(qs[...], kk, NT, preferred_element_type=F32)
        return _sb_head(z, vv, later_mat, suffix, mask)

    row = lax.broadcasted_iota(jnp.int32, (nh * tq, tq), 0) % tq
    strict = lax.broadcasted_iota(jnp.int32, (nh * tq, tq), 1) < row
    o, suffix = step(qi, jnp.zeros((nh * tq, 1), F32), strict)
    acc[...] = o

    def body(it, suffix):
        o, suffix = step(qi - 1 - it, suffix, None)
        acc[...] += o
        return suffix

    lax.fori_loop(0, qi, body, suffix)
    for p in range(nh // 2):
        o_ref[:, p * LANES:(p + 1) * LANES] = jnp.where(
            low, acc[(2 * p) * tq:(2 * p + 1) * tq, :], acc[(2 * p + 1) * tq:(2 * p + 2) * tq, :]).astype(o_ref.dtype)


def _sb_prompt(z, B, T):
    tq = _pick(T, 256, 16)
    nq = T // tq
    nh = 16
    w = nh * SB_DK
    return pl.pallas_call(
        functools.partial(_sb_prompt_kernel, tq=tq, nh=nh),
        grid=(B, nq, SB_HEADS // nh),
        in_specs=[pl.BlockSpec((tq, w), lambda b, i, p: (b * nq + i, C_SBQ // w + p)),
                  pl.BlockSpec((T, LANES), lambda b, i, p: (b, C_M1 // LANES)),
                  pl.BlockSpec((T, LANES), lambda b, i, p: (b, C_M2 // LANES))],
        out_specs=pl.BlockSpec((tq, w), lambda b, i, p: (b * nq + i, p)),
        out_shape=jax.ShapeDtypeStruct((B * T, SB_HEADS * SB_DV), BF16),
        scratch_shapes=[pltpu.VMEM((nh * tq, LANES), BF16), pltpu.VMEM((nh * tq, LANES), F32)],
        compiler_params=_cp(("parallel", "parallel", "arbitrary")),
        name="sb_prompt",
    )(z, z, z)


def _sb_sample_kernel(pt_ref, l_ref, q_ref, kn_ref, vn_ref, *rest, P, nq):
    k_refs, v_refs = rest[:P], rest[P:2 * P]
    o_ref, qs, acc, suf, kb, vb = rest[2 * P:]
    g = pl.program_id(1)
    R = SB_HEADS * nq
    later_mat = _later_matrix(PAGE_SIZE)

    @pl.when(g == 0)
    def _():
        x = q_ref[...] * SB_SCALE
        pieces = []
        for h in range(SB_HEADS):
            blk = x[:, (h // 2) * LANES:(h // 2 + 1) * LANES]
            if h % 2:
                blk = pltpu.roll(blk, SB_DK, 1)
            pieces.append(blk[:, :SB_DK])
        qs[...] = jnp.concatenate(pieces, axis=0).astype(BF16)
        pad = PAGE_SIZE - nq
        k_new = pltpu.roll(kn_ref[...], SB_DK, 1)[:, :SB_DK]
        k_new = jnp.concatenate([k_new, jnp.zeros((pad, SB_DK), F32)], axis=0).astype(BF16)
        v_new = jnp.concatenate([vn_ref[:, :SB_DV], jnp.zeros((pad, SB_DV), F32)], axis=0).astype(BF16)
        t = lax.broadcasted_iota(jnp.int32, (R, PAGE_SIZE), 0) % nq
        key = lax.broadcasted_iota(jnp.int32, (R, PAGE_SIZE), 1)
        z = lax.dot_general(qs[...], k_new, NT, preferred_element_type=F32)
        o, s = _sb_head(z, v_new, later_mat, jnp.zeros((R, 1), F32), key < t)
        acc[...] = o
        suf[...] = s

    for i in range(P):
        kb[i * PAGE_SIZE:(i + 1) * PAGE_SIZE, :] = k_refs[i][...].astype(BF16)
        vb[i * PAGE_SIZE:(i + 1) * PAGE_SIZE, :] = v_refs[i][...].astype(BF16)
    pb = 2 * PAGE_SIZE
    rank = lambda c: (1 - c // PAGE_SIZE) * PAGE_SIZE + c % PAGE_SIZE
    later2 = (rank(lax.broadcasted_iota(jnp.int32, (pb, pb), 0))
              > rank(lax.broadcasted_iota(jnp.int32, (pb, pb), 1))).astype(BF16)
    z = lax.dot_general(qs[...], kb[...], NT, preferred_element_type=F32)
    ls = _log_sigmoid(z)
    lsn = ls - z
    hi, lo = _split_bf16(lsn)
    prefix = suf[...]
    pieces = []
    for j in range(P // 2):
        cols = slice(j * pb, (j + 1) * pb)
        later = (jnp.dot(hi[:, cols], later2, preferred_element_type=F32)
                 + jnp.dot(lo[:, cols], later2, preferred_element_type=F32))
        pieces.append(later + prefix)
        oldest = j * pb + PAGE_SIZE
        prefix = prefix + later[:, PAGE_SIZE:PAGE_SIZE + 1] + lsn[:, oldest:oldest + 1]
    w = jnp.exp(ls + jnp.concatenate(pieces, axis=1))
    acc[...] += jnp.dot(w.astype(BF16), vb[...], preferred_element_type=F32)
    suf[...] = prefix

    @pl.when(g == pl.num_programs(1) - 1)
    def _():
        o_ref[...] = acc[...]


def _sb_sample(l, pt, z, cache_k, cache_v, Bs, nq):
    NP = pt.shape[0] // Bs
    P = _pick(NP, 16, 2)
    R = SB_HEADS * nq

    def page(width, i):
        return pl.BlockSpec((None, None, PAGE_SIZE, width),
                            lambda b, g, pt, l: (l[0], pt[b * NP + NP - 1 - (g * P + i)], 0, 0))

    out = pl.pallas_call(
        functools.partial(_sb_sample_kernel, P=P, nq=nq),
        grid_spec=pltpu.PrefetchScalarGridSpec(
            num_scalar_prefetch=2, grid=(Bs, NP // P),
            in_specs=[pl.BlockSpec((nq, SB_HEADS * SB_DK), lambda b, g, pt, l: (b, C_SBQ // (SB_HEADS * SB_DK))),
                      pl.BlockSpec((nq, LANES), lambda b, g, pt, l: (b, C_M1 // LANES)),
                      pl.BlockSpec((nq, LANES), lambda b, g, pt, l: (b, C_M2 // LANES))]
            + [page(SB_DK, i) for i in range(P)] + [page(SB_DV, i) for i in range(P)],
            out_specs=pl.BlockSpec((None, R, SB_DV), lambda b, g, pt, l: (b, 0, 0)),
            scratch_shapes=[pltpu.VMEM((R, SB_DK), BF16), pltpu.VMEM((R, SB_DV), F32), pltpu.VMEM((R, 1), F32),
                            pltpu.VMEM((P * PAGE_SIZE, SB_DK), BF16), pltpu.VMEM((P * PAGE_SIZE, SB_DV), BF16)]),
        out_shape=jax.ShapeDtypeStruct((Bs, R, SB_DV), F32),
        compiler_params=_cp(("parallel", "arbitrary")),
        name="sb_sample",
    )(pt, l, z, z, z, *([cache_k] * P), *([cache_v] * P))
    return out.reshape(Bs, SB_HEADS, nq, SB_DV).transpose(0, 2, 1, 3).reshape(Bs * nq, SB_HEADS * SB_DV)


def _gla_kernel(l_ref, q_ref, k_ref, v_ref, gl_ref, gr_ref, wg_ref, bg_ref, gn_ref, *rest, C, chained):
    if chained:
        o_ref, sout_ref, S = rest

        @pl.when(pl.program_id(1) == 0)
        def _():
            S[...] = jnp.zeros(S.shape, F32)
    else:
        s0_ref, o_ref, sout_ref = rest
    tril = lax.broadcasted_iota(jnp.int32, (C, C), 0) >= lax.broadcasted_iota(jnp.int32, (C, C), 1)
    csum = tril.astype(BF16)
    ones = jnp.ones((C, GLA_DK), BF16)

    def chunk(c, carry):
        rows = pl.ds(pl.multiple_of(c * C, C), C)
        q_all = q_ref[rows, :] * GLA_SCALE
        k_all = k_ref[rows, :]
        v_all = v_ref[rows, :].astype(BF16)
        gr_all = gr_ref[rows, :]
        pre = jnp.dot(gl_ref[rows, :].astype(BF16), wg_ref[...], preferred_element_type=F32) + bg_ref[...]
        hi_all, lo_all = _split_bf16(_log_sigmoid(pre) / GLA_TAU)
        b_all = (jnp.dot(csum, hi_all, preferred_element_type=F32)
                 + jnp.dot(csum, lo_all, preferred_element_type=F32))
        for h in range(GLA_HEADS):
            kcols = slice(h * GLA_DK, (h + 1) * GLA_DK)
            vcols = slice(h * GLA_DV, (h + 1) * GLA_DV)
            qc, kc, vc, b = q_all[:, kcols], k_all[:, kcols], v_all[:, vcols], b_all[:, kcols]
            hi, lo = hi_all[:, kcols], lo_all[:, kcols]
            b_tot = (lax.dot_general(hi, ones, TN, preferred_element_type=F32)
                     + lax.dot_general(lo, ones, TN, preferred_element_type=F32))
            b_last = b[C - 1:C, :]
            qi = (qc * jnp.exp(b)).astype(BF16)
            att = lax.dot_general(qi, (kc * jnp.exp(-b)).astype(BF16), NT, preferred_element_type=F32)
            att = jnp.where(tril, att, 0.0)
            s_old = S[h] if chained else s0_ref[c, h]
            o = (jnp.dot(att.astype(BF16), vc, preferred_element_type=F32)
                 + jnp.dot(qi, s_old.astype(BF16), preferred_element_type=F32))
            kd = (kc * jnp.exp(b_last - b)).astype(BF16)
            decay = jnp.exp(b_tot)
            decay = jnp.concatenate([decay] * (GLA_DV // GLA_DK), axis=1)
            s_new = decay * s_old + lax.dot_general(kd, vc, TN, preferred_element_type=F32)
            if chained:
                S[h] = s_new
            else:
                sout_ref[c, h] = s_new
            on = o * lax.rsqrt(jnp.mean(o * o, axis=-1, keepdims=True) + NORM_EPS) * gn_ref[...]
            gr = gr_all[:, vcols]
            o_ref[rows, vcols] = (on * (gr * jax.nn.sigmoid(gr))).astype(o_ref.dtype)
        return carry

    lax.fori_loop(0, q_ref.shape[0] // C, chunk, 0)
    if chained:
        sout_ref[...] = S[...]


def _gla(l, z, W, B, T, state):
    C = math.gcd(T, GLA_CHUNK)
    chained = state is None
    HK, HV = GLA_HEADS * GLA_DK, GLA_HEADS * GLA_DV
    par = lambda a: pl.BlockSpec((None,) + a.shape[1:], lambda *g: (g[-1][0],) + (0,) * (a.ndim - 1))
    params = [W['w_gla_gate'], W['b_gla_gate'], W['gla_norm']]
    st_shape = jax.ShapeDtypeStruct((B, GLA_HEADS, GLA_DK, GLA_DV), F32)
    if chained:
        tt = _pick(T, 512, C)
        nt = T // tt
        grid = (B, nt)
        cb = lambda w, c0: pl.BlockSpec((tt, w), lambda b, t, l: (b * nt + t, c0 // w))
        in_specs = [cb(HK, C_GQ), cb(HK, C_GK), cb(HV, C_GV), cb(LANES, C_M2), cb(HV, C_GR)] + [par(p) for p in params]
        args = [z] * 5 + params
        st_spec = pl.BlockSpec((None, GLA_HEADS, GLA_DK, GLA_DV), lambda b, t, l: (b, 0, 0, 0))
        o_spec = pl.BlockSpec((tt, HV), lambda b, t, l: (b * nt + t, 0))
        scratch = [pltpu.VMEM((GLA_HEADS, GLA_DK, GLA_DV), F32)]
        sem = ("parallel", "arbitrary")
    else:
        assert T == C
        nb = _pick(B, 8, 1)
        grid = (B // nb,)
        cb = lambda w, c0: pl.BlockSpec((nb * T, w), lambda i, l: (i, c0 // w))
        st_spec = pl.BlockSpec((nb, GLA_HEADS, GLA_DK, GLA_DV), lambda i, l: (i, 0, 0, 0))
        in_specs = ([cb(HK, C_GQ), cb(HK, C_GK), cb(HV, C_GV), cb(LANES, C_M2), cb(HV, C_GR)] + [par(p) for p in params]
                    + [pl.BlockSpec((None, nb, GLA_HEADS, GLA_DK, GLA_DV), lambda i, l: (l[0], i, 0, 0, 0))])
        args = [z] * 5 + params + [state]
        o_spec = pl.BlockSpec((nb * T, HV), lambda i, l: (i, 0))
        scratch = []
        sem = ("parallel",)
    return pl.pallas_call(
        functools.partial(_gla_kernel, C=C, chained=chained),
        grid_spec=pltpu.PrefetchScalarGridSpec(
            num_scalar_prefetch=1, grid=grid, in_specs=in_specs,
            out_specs=[o_spec, st_spec], scratch_shapes=scratch),
        out_shape=[jax.ShapeDtypeStruct((B * T, HV), BF16 if chained else F32), st_shape],
        compiler_params=_cp(sem),
        name="gla_prompt" if chained else "gla_sample",
    )(l, *args)


def _s5_kernel(l_ref, u_ref, bre_ref, bim_ref, pw_ref, cre_ref, cim_ref, d_ref, wglu_ref, bglu_ref, *rest,
               tt, chained):
    if chained:
        o_ref, sre_ref, sim_ref, xre, xim, car_re, car_im = rest
    else:
        x0re_ref, x0im_ref, o_ref, sre_ref, sim_ref, xre, xim = rest
    slab_u = S5_WIDTH // S5_KT
    slab_x = S5_NSTATE // S5_KT
    u = u_ref[...]
    ub = u.astype(BF16)
    for kt in range(S5_KT):
        us = ub[:, kt * slab_u:(kt + 1) * slab_u]
        xre[:, kt * slab_x:(kt + 1) * slab_x] = jnp.dot(us, bre_ref[kt], preferred_element_type=F32)
        xim[:, kt * slab_x:(kt + 1) * slab_x] = jnp.dot(us, bim_ref[kt], preferred_element_type=F32)

    if chained:
        @pl.when(pl.program_id(1) == 0)
        def _():
            car_re[...] = jnp.zeros(car_re.shape, F32)
            car_im[...] = jnp.zeros(car_im.shape, F32)

    def tile(j, carry):
        rows = pl.ds(pl.multiple_of(j * SCAN_ROWS, SCAN_ROWS), SCAN_ROWS)
        for kt in range(S5_KT):
            cols = slice(kt * slab_x, (kt + 1) * slab_x)
            ar, ai = xre[rows, cols], xim[rows, cols]
            for n, k in enumerate((1, 2, 4)):
                pr, pi = pw_ref[2 * n, :, cols], pw_ref[2 * n + 1, :, cols]
                sr, si = pltpu.roll(ar, k, 0), pltpu.roll(ai, k, 0)
                ar, ai = ar + pr * sr - pi * si, ai + pr * si + pi * sr
            if chained:
                cr, ci = car_re[:, cols], car_im[:, cols]
            else:
                cr = jnp.broadcast_to(x0re_ref[pl.ds(j, 1), cols], (SCAN_ROWS, slab_x))
                ci = jnp.broadcast_to(x0im_ref[pl.ds(j, 1), cols], (SCAN_ROWS, slab_x))
            pr, pi = pw_ref[6, :, cols], pw_ref[7, :, cols]
            ar, ai = ar + pr * cr - pi * ci, ai + pr * ci + pi * cr
            xre[rows, cols] = ar
            xim[rows, cols] = ai
            last_r, last_i = ar[SCAN_ROWS - 1:SCAN_ROWS, :], ai[SCAN_ROWS - 1:SCAN_ROWS, :]
            if chained:
                car_re[:, cols] = jnp.broadcast_to(last_r, (SCAN_ROWS, slab_x))
                car_im[:, cols] = jnp.broadcast_to(last_i, (SCAN_ROWS, slab_x))
            else:
                sre_ref[pl.ds(j, 1), cols] = last_r
                sim_ref[pl.ds(j, 1), cols] = last_i
        return carry

    lax.fori_loop(0, tt // SCAN_ROWS, tile, 0)
    if chained:
        sre_ref[...] = car_re[0:1, :]
        sim_ref[...] = car_im[0:1, :]

    ys = []
    for kt in range(S5_KT):
        cols = slice(kt * slab_x, (kt + 1) * slab_x)
        ys.append(jnp.dot(xre[:, cols].astype(BF16), cre_ref[kt], preferred_element_type=F32)
                  - jnp.dot(xim[:, cols].astype(BF16), cim_ref[kt], preferred_element_type=F32))
    y = jnp.concatenate(ys, axis=1) + d_ref[...] * u
    y = jax.nn.gelu(y)
    gate = jax.nn.sigmoid(jnp.dot(y.astype(BF16), wglu_ref[...], preferred_element_type=F32) + bglu_ref[...])
    o_ref[...] = (y * gate).astype(o_ref.dtype)


def _s5(l, z, W, B, T, state):
    chained = state is None
    M = B * T
    par = lambda a: pl.BlockSpec((None,) + a.shape[1:], lambda *g: (g[-1][0],) + (0,) * (a.ndim - 1))
    params = [W['s5_bre'], W['s5_bim'], W['s5_pw'], W['s5_cre'], W['s5_cim'], W['s5_d'], W['w_s5_glu'], W['b_s5_glu']]
    if chained:
        tt = _pick(T, 256, SCAN_ROWS)
        nt = T // tt
        grid = (B, nt)
        u_spec = pl.BlockSpec((tt, S5_WIDTH), lambda b, t, l: (b * nt + t, C_S5 // S5_WIDTH))
        o_spec = pl.BlockSpec((tt, S5_WIDTH), lambda b, t, l: (b * nt + t, 0))
        st_spec = pl.BlockSpec((None, 1, S5_NSTATE), lambda b, t, l: (b, 0, 0))
        st_shape = jax.ShapeDtypeStruct((B, 1, S5_NSTATE), F32)
        in_specs = [u_spec] + [par(p) for p in params]
        args = [z] + params
        scratch = [pltpu.VMEM((tt, S5_NSTATE), F32), pltpu.VMEM((tt, S5_NSTATE), F32),
                   pltpu.VMEM((SCAN_ROWS, S5_NSTATE), F32), pltpu.VMEM((SCAN_ROWS, S5_NSTATE), F32)]
        sem = ("parallel", "arbitrary")
    else:
        assert T == SCAN_ROWS
        tt = _pick(M, 256, 8 * SCAN_ROWS)
        grid = (M // tt,)
        nb = tt // SCAN_ROWS
        u_spec = pl.BlockSpec((tt, S5_WIDTH), lambda i, l: (i, C_S5 // S5_WIDTH))
        o_spec = pl.BlockSpec((tt, S5_WIDTH), lambda i, l: (i, 0))
        st_spec = pl.BlockSpec((nb, S5_NSTATE), lambda i, l: (i, 0))
        st_shape = jax.ShapeDtypeStruct((B, S5_NSTATE), F32)
        x0_spec = pl.BlockSpec((None, nb, S5_NSTATE), lambda i, l: (l[0], i, 0))
        in_specs = [u_spec] + [par(p) for p in params] + [x0_spec, x0_spec]
        args = [z] + params + list(state)
        scratch = [pltpu.VMEM((tt, S5_NSTATE), F32), pltpu.VMEM((tt, S5_NSTATE), F32)]
        sem = ("parallel",)
    return pl.pallas_call(
        functools.partial(_s5_kernel, tt=tt, chained=chained),
        grid_spec=pltpu.PrefetchScalarGridSpec(
            num_scalar_prefetch=1, grid=grid, in_specs=in_specs,
            out_specs=[o_spec, st_spec, st_spec], scratch_shapes=scratch),
        out_shape=[jax.ShapeDtypeStruct((M, S5_WIDTH), BF16 if chained else F32), st_shape, st_shape],
        compiler_params=_cp(sem),
        name="s5_prompt" if chained else "s5_sample",
    )(l, *args)


def _rope_tables(pos):
    half = MLA_ROPE // 2
    inv = ROPE_THETA ** (-jnp.arange(half, dtype=F32) / half)
    ang = pos.astype(F32)[:, None] * inv[None, :]
    cos, sin = jnp.cos(ang), jnp.sin(ang)
    zero = jnp.zeros_like(cos)
    pad = jnp.zeros((pos.shape[0], LANES - MLA_ROPE), F32)
    return (jnp.concatenate([cos, cos, pad], axis=1),
            jnp.concatenate([-sin, zero, pad], axis=1),
            jnp.concatenate([zero, sin, pad], axis=1))


def _s5_tables(a_re, a_im, log_dt, b_re, b_im):
    L = a_re.shape[0]
    dt = jnp.exp(log_dt)[..., None]
    mag = jnp.exp(a_re * dt)
    ab_re, ab_im = mag * jnp.cos(a_im * dt), mag * jnp.sin(a_im * dt)
    den = a_re * a_re + a_im * a_im
    f_re = ((ab_re - 1.0) * a_re + ab_im * a_im) / den
    f_im = (ab_im * a_re - (ab_re - 1.0) * a_im) / den
    bb_re = f_re[..., None] * b_re - f_im[..., None] * b_im
    bb_im = f_re[..., None] * b_im + f_im[..., None] * b_re
    ar, ai = ab_re.reshape(L, S5_NSTATE), ab_im.reshape(L, S5_NSTATE)
    pows = [(ar, ai)]
    for _ in range(SCAN_ROWS - 1):
        pr, pi = pows[-1]
        pows.append((pr * ar - pi * ai, pr * ai + pi * ar))
    idx = jnp.arange(SCAN_ROWS)[None, :, None]
    tabs = []
    for k in (1, 2, 4):
        pr, pi = pows[k - 1]
        tabs.append(jnp.where(idx >= k, pr[:, None, :], 0.0))
        tabs.append(jnp.where(idx >= k, pi[:, None, :], 0.0))
    tabs.append(jnp.stack([p[0] for p in pows], axis=1))
    tabs.append(jnp.stack([p[1] for p in pows], axis=1))
    return bb_re, bb_im, jnp.stack(tabs, axis=1)


def _block_diag_in(bb):
    L = bb.shape[0]
    gpk = S5_GROUPS // S5_KT
    x = bb.reshape(L, S5_KT, gpk, S5_STATE, S5_GROUP)
    eye = jnp.eye(gpk, dtype=bb.dtype)
    y = jnp.einsum('lkgnc,gh->lkgchn', x, eye)
    return y.reshape(L, S5_KT, gpk * S5_GROUP, gpk * S5_STATE).astype(BF16)


def _block_diag_out(c):
    L = c.shape[0]
    gpk = S5_GROUPS // S5_KT
    x = c.reshape(L, S5_KT, gpk, S5_GROUP, S5_STATE)
    eye = jnp.eye(gpk, dtype=c.dtype)
    y = jnp.einsum('lkgcn,gh->lkgnhc', x, eye)
    return y.reshape(L, S5_KT, gpk * S5_STATE, gpk * S5_GROUP).astype(BF16)


def _prepare_weights(p):
    L = p['w_in'].shape[0]
    D = p['w_in'].shape[1]
    W = {}
    offs, o = [], 0
    for n in (Q_LORA, KV_LORA, MLA_ROPE, SB_HEADS * SB_DK, SB_DK, SB_DV, GLA_HEADS * GLA_DK, GLA_HEADS * GLA_DK,
              GLA_HEADS * GLA_DV, GLA_GATE_RANK, GLA_HEADS * GLA_DV, S5_WIDTH):
        offs.append((o, o + n))
        o += n
    (cq, ckv, kr, sbq, sbk, sbv, gq, gk, gv, glow, gr, s5u) = [p['w_in'][:, :, a:b] for a, b in offs]
    zpad = jnp.zeros((L, D, LANES - SB_DV - GLA_GATE_RANK), F32)
    W['w_in'] = jnp.concatenate([sbq, gv, gr, s5u, gq, cq, ckv, gk, kr, sbk, sbv, glow, zpad], axis=2).astype(BF16)
    wq = p['w_uq'].reshape(L, Q_LORA, MLA_HEADS, MLA_NOPE + MLA_ROPE)
    wq = jnp.concatenate([wq, jnp.zeros((L, Q_LORA, MLA_HEADS, MLA_QW - MLA_NOPE - MLA_ROPE), F32)], axis=3)
    W['w_uq'] = wq.reshape(L, Q_LORA, MLA_HEADS * MLA_QW).astype(BF16)
    wkv = p['w_ukv'].astype(BF16)
    W['w_ukv'] = wkv
    wkv4 = wkv.reshape(L, KV_LORA, MLA_HEADS, MLA_NOPE + MLA_V)
    W['wk_t'] = wkv4[..., :MLA_NOPE].transpose(0, 2, 3, 1).reshape(L, MLA_HEADS * MLA_NOPE, KV_LORA)
    W['wv'] = wkv4[..., MLA_NOPE:].transpose(0, 2, 1, 3)
    row = lambda a: a.reshape(L, 1, a.shape[-1])
    padr = lambda a: jnp.concatenate([a, jnp.zeros((L, LANES - MLA_ROPE), F32)], axis=1)
    for k in ('norm_mix', 'norm_cq', 'norm_ckv', 'qn_nope', 'kn_nope', 'b_gla_gate', 'gla_norm', 's5_d', 'b_s5_glu',
              'b_gates', 'norm_ffn'):
        W[k] = row(p[k])
    W['qn_rope'] = row(padr(p['qn_rope']))
    W['kn_rope'] = row(padr(p['kn_rope']))
    wg = jnp.zeros((L, LANES, GLA_HEADS * GLA_DK), F32)
    W['w_gla_gate'] = wg.at[:, SB_DV:SB_DV + GLA_GATE_RANK, :].set(p['w_gla_gate']).astype(BF16)
    bb_re, bb_im, W['s5_pw'] = _s5_tables(p['s5_a_re'], p['s5_a_im'], p['s5_log_dt'], p['s5_b_re'], p['s5_b_im'])
    W['s5_bre'], W['s5_bim'] = _block_diag_in(bb_re), _block_diag_in(bb_im)
    W['s5_cre'], W['s5_cim'] = _block_diag_out(p['s5_c_re']), _block_diag_out(p['s5_c_im'])
    for k in ('w_s5_glu', 'w_gates', 'w_branch', 'w_out', 'w_ff_gate', 'w_ff_up', 'w_ff_down'):
        W[k] = p[k].astype(BF16)
    return W


def _trunk_layer(l, h, W, tabs, B, T, past):
    sample = past is not None
    hn = _rmsnorm(l, h, W['norm_mix'])
    z = _matmul(l, hn, W['w_in'], tn_target=640)
    if sample:
        pt, cache_lat, cache_kr, cache_k, cache_v, st_gla, st_re, st_im = past
        q_att, q_abs, lat, kr = _mla_prep(l, z, tabs, W, sample=True)
        out_a = _mla_sample(l, pt, q_abs, q_att, cache_lat, cache_kr, lat, kr, W, B, T)
        out_b = _sb_sample(l, pt, z, cache_k, cache_v, B, T)
        out_c, gla_s = _gla(l, z, W, B, T, st_gla)
        out_d, s5_re, s5_im = _s5(l, z, W, B, T, (st_re, st_im))
    else:
        q_att, lat, kr, k_att, v_att = _mla_prep(l, z, tabs, W, sample=False)
        out_a = _mla_prompt(q_att, k_att, v_att, B, T)
        out_b = _sb_prompt(z, B, T)
        out_c, gla_s = _gla(l, z, W, B, T, None)
        out_d, s5_re, s5_im = _s5(l, z, W, B, T, None)
    merged = _merge(l, hn, W['w_gates'], W['b_gates'], (out_a, out_b, out_c, out_d), W['w_branch'])
    h = _matmul(l, merged, W['w_out'], res=h)
    hn2 = _rmsnorm(l, h, W['norm_ffn'])
    act = _ffn_gate_up(l, hn2, W['w_ff_gate'], W['w_ff_up'])
    h = _matmul(l, act, W['w_ff_down'], res=h, tm_target=512, tn_target=256)
    states = (lat.reshape(B, T, KV_LORA),
              kr[:, :MLA_ROPE].reshape(B, T, MLA_ROPE),
              z[:, C_M1 + MLA_ROPE:C_M1 + MLA_ROPE + SB_DK].reshape(B, T, 1, SB_DK),
              z[:, C_M2:C_M2 + SB_DV].reshape(B, T, 1, SB_DV),
              gla_s,
              s5_re.reshape(B, S5_GROUPS, S5_STATE),
              s5_im.reshape(B, S5_GROUPS, S5_STATE))
    return h, states


def kernel(x_prompt, x_sample, cache_mla_latent, cache_mla_krope, cache_sb_k, cache_sb_v, state_gla, state_s5_re, state_s5_im, page_table, norm_mix, w_in, norm_cq, w_uq, norm_ckv, w_ukv, qn_nope, qn_rope, kn_nope, kn_rope, w_gla_gate, b_gla_gate, gla_norm, s5_a_re, s5_a_im, s5_log_dt, s5_b_re, s5_b_im, s5_c_re, s5_c_im, s5_d, w_s5_glu, b_s5_glu, w_gates, b_gates, w_branch, w_out, norm_ffn, w_ff_gate, w_ff_up, w_ff_down):
    L = w_in.shape[0]
    B, T, D = x_prompt.shape
    Bs, Ts, _ = x_sample.shape
    n_pool, page = cache_mla_latent.shape[1], cache_mla_latent.shape[2]
    past_len = page_table.shape[1] * page
    W = _prepare_weights(dict(
        norm_mix=norm_mix, w_in=w_in, norm_cq=norm_cq, w_uq=w_uq, norm_ckv=norm_ckv, w_ukv=w_ukv,
        qn_nope=qn_nope, qn_rope=qn_rope, kn_nope=kn_nope, kn_rope=kn_rope,
        w_gla_gate=w_gla_gate, b_gla_gate=b_gla_gate, gla_norm=gla_norm,
        s5_a_re=s5_a_re, s5_a_im=s5_a_im, s5_log_dt=s5_log_dt, s5_b_re=s5_b_re, s5_b_im=s5_b_im,
        s5_c_re=s5_c_re, s5_c_im=s5_c_im, s5_d=s5_d, w_s5_glu=w_s5_glu, b_s5_glu=b_s5_glu,
        w_gates=w_gates, b_gates=b_gates, w_branch=w_branch, w_out=w_out, norm_ffn=norm_ffn,
        w_ff_gate=w_ff_gate, w_ff_up=w_ff_up, w_ff_down=w_ff_down))
    tabs_p = _rope_tables(jnp.arange(T))
    reps = _pick(Bs * Ts, 256, 8) // Ts
    tabs_s = tuple(jnp.tile(t, (reps, 1)) for t in _rope_tables(past_len + jnp.arange(Ts)))
    pt = page_table.reshape(-1).astype(jnp.int32)
    cache_k = cache_sb_k.reshape(L, n_pool, page, SB_DK)
    cache_v = cache_sb_v.reshape(L, n_pool, page, SB_DV)
    st_re = state_s5_re.reshape(L, Bs, S5_NSTATE)
    st_im = state_s5_im.reshape(L, Bs, S5_NSTATE)
    past = (pt, cache_mla_latent, cache_mla_krope, cache_k, cache_v, state_gla, st_re, st_im)

    def layer(carry, li):
        hp, hs = carry
        l = li.reshape(1)
        hp, st_p = _trunk_layer(l, hp, W, tabs_p, B, T, None)
        hs, st_s = _trunk_layer(l, hs, W, tabs_s, Bs, Ts, past)
        return (hp, hs), (st_p, st_s)

    (hp, hs), (sp, ss) = lax.scan(layer, (x_prompt.reshape(B * T, D), x_sample.reshape(Bs * Ts, D)),
                                  jnp.arange(L, dtype=jnp.int32))
    return (hp.reshape(B, T, D), hs.reshape(Bs, Ts, D)) + tuple(sp) + tuple(ss)
```

```python
import functools
import math

import jax
import jax.numpy as jnp
from jax import lax
from jax.experimental import pallas as pl
from jax.experimental.pallas import tpu as pltpu

D_MODEL = 4096
DEPTH = 4
PAGE_SIZE = 128
MLA_HEADS = 8
MLA_NOPE = 128
MLA_ROPE = 64
MLA_V = 128
Q_LORA = 768
KV_LORA = 256
SB_HEADS = 16
SB_DK = 64
SB_DV = 64
GLA_HEADS = 4
GLA_DK = 128
GLA_DV = 256
GLA_GATE_RANK = 16
GLA_TAU = 16.0
GLA_CHUNK = 32
S5_WIDTH = 1024
S5_GROUP = 16
S5_GROUPS = S5_WIDTH // S5_GROUP
S5_STATE = 64
N_BRANCH = 4
BRANCH_WIDTH = 1024
ROPE_THETA = 10000.0
NORM_EPS = 1e-6
NEG_INF = -1e30
MLA_SCALE = 1.0 / math.sqrt(MLA_NOPE + MLA_ROPE)
SB_SCALE = 1.0 / math.sqrt(SB_DK)
GLA_SCALE = 1.0 / math.sqrt(GLA_DK)

F32 = jnp.float32
BF16 = jnp.bfloat16
LANES = 128
VMEM_LIMIT_MB = 56
MLA_QW = 256
S5_NSTATE = S5_GROUPS * S5_STATE
S5_KT = 4
SCAN_ROWS = 8
ROW_CHUNK = 1024

C_SBQ = 0
C_GV = 1024
C_GR = 2048
C_S5 = 3072
C_GQ = 4096
C_CQ = 4608
C_CKV = 5376
C_GK = 5632
C_M1 = 6144
C_M2 = 6272
Z_COLS = 6400

NT = (((1,), (1,)), ((), ()))
TN = (((0,), (0,)), ((), ()))


def _cp(sem):
    return pltpu.CompilerParams(dimension_semantics=sem, vmem_limit_bytes=VMEM_LIMIT_MB * 2**20)


def _pick(n, target, mult):
    t = (min(n, target) // mult) * mult
    while t >= mult:
        if n % t == 0:
            return t
        t -= mult
    return n


def _log_sigmoid(x):
    return jnp.minimum(x, 0.0) - jnp.log(1.0 + jnp.exp(-jnp.abs(x)))


def _split_bf16(x):
    hi = x.astype(BF16)
    lo = (x - hi.astype(F32)).astype(BF16)
    return hi, lo


def _rmsnorm_kernel(l_ref, x_ref, g_ref, o_ref):
    x = x_ref[...]
    y = x * lax.rsqrt(jnp.mean(x * x, axis=-1, keepdims=True) + NORM_EPS)
    o_ref[...] = (y * g_ref[...]).astype(o_ref.dtype)


def _rmsnorm(l, x, g):
    M, D = x.shape
    tm = _pick(M, 256, 8)
    return pl.pallas_call(
        _rmsnorm_kernel,
        grid_spec=pltpu.PrefetchScalarGridSpec(
            num_scalar_prefetch=1, grid=(M // tm,),
            in_specs=[pl.BlockSpec((tm, D), lambda i, l: (i, 0)),
                      pl.BlockSpec((None, 1, D), lambda i, l: (l[0], 0, 0))],
            out_specs=pl.BlockSpec((tm, D), lambda i, l: (i, 0))),
        out_shape=jax.ShapeDtypeStruct((M, D), BF16),
        compiler_params=_cp(("parallel",)),
        name="rmsnorm",
    )(l, x, g)


def _mm_kernel(l_ref, a_ref, w_ref, *rest, cm, has_res):
    if has_res:
        r_ref, o_ref = rest
    else:
        (o_ref,) = rest

    def chunk(c, carry):
        rows = pl.ds(pl.multiple_of(c * cm, cm), cm)
        acc = jnp.dot(a_ref[rows, :].astype(BF16), w_ref[...], preferred_element_type=F32)
        if has_res:
            acc = acc + r_ref[rows, :]
        o_ref[rows, :] = acc.astype(o_ref.dtype)
        return carry

    lax.fori_loop(0, a_ref.shape[0] // cm, chunk, 0)


def _matmul(l, a, w, *, res=None, out_dtype=F32, tm_target=1024, tn_target=512):
    M, K = a.shape
    N = w.shape[-1]
    tm = _pick(M, tm_target, 8)
    tn = _pick(N, tn_target, LANES)
    cm = _pick(tm, ROW_CHUNK, 8)
    in_specs = [pl.BlockSpec((tm, K), lambda i, j, l: (i, 0)),
                pl.BlockSpec((None, K, tn), lambda i, j, l: (l[0], 0, j))]
    args = [a, w]
    if res is not None:
        in_specs.append(pl.BlockSpec((tm, tn), lambda i, j, l: (i, j)))
        args.append(res)
    return pl.pallas_call(
        functools.partial(_mm_kernel, cm=cm, has_res=res is not None),
        grid_spec=pltpu.PrefetchScalarGridSpec(
            num_scalar_prefetch=1, grid=(M // tm, N // tn),
            in_specs=in_specs,
            out_specs=pl.BlockSpec((tm, tn), lambda i, j, l: (i, j))),
        out_shape=jax.ShapeDtypeStruct((M, N), out_dtype),
        compiler_params=_cp(("parallel", "arbitrary")),
        name="matmul",
    )(l, *args)


def _ffn_gu_kernel(l_ref, a_ref, wg_ref, wu_ref, o_ref, *, cm):
    def chunk(c, carry):
        rows = pl.ds(pl.multiple_of(c * cm, cm), cm)
        a = a_ref[rows, :]
        g = jnp.dot(a, wg_ref[...], preferred_element_type=F32)
        u = jnp.dot(a, wu_ref[...], preferred_element_type=F32)
        o_ref[rows, :] = (g * jax.nn.sigmoid(g) * u).astype(o_ref.dtype)
        return carry

    lax.fori_loop(0, a_ref.shape[0] // cm, chunk, 0)


def _ffn_gate_up(l, a, wg, wu):
    M, K = a.shape
    N = wg.shape[-1]
    tm = _pick(M, 1024, 8)
    tn = _pick(N, 256, LANES)
    cm = _pick(tm, ROW_CHUNK, 8)
    return pl.pallas_call(
        functools.partial(_ffn_gu_kernel, cm=cm),
        grid_spec=pltpu.PrefetchScalarGridSpec(
            num_scalar_prefetch=1, grid=(M // tm, N // tn),
            in_specs=[pl.BlockSpec((tm, K), lambda i, j, l: (i, 0)),
                      pl.BlockSpec((None, K, tn), lambda i, j, l: (l[0], 0, j)),
                      pl.BlockSpec((None, K, tn), lambda i, j, l: (l[0], 0, j))],
            out_specs=pl.BlockSpec((tm, tn), lambda i, j, l: (i, j))),
        out_shape=jax.ShapeDtypeStruct((M, N), BF16),
        compiler_params=_cp(("parallel", "arbitrary")),
        name="ffn_gate_up",
    )(l, a, wg, wu)


def _merge_kernel(l_ref, hn_ref, wg_ref, bg_ref, b0_ref, b1_ref, b2_ref, b3_ref, wb_ref, o_ref, acc_ref, *, cm):
    bi = pl.program_id(2)
    n_chunks = hn_ref.shape[0] // cm

    def run(br_ref, first, last):
        def chunk(c, carry):
            rows = pl.ds(pl.multiple_of(c * cm, cm), cm)
            g = jnp.dot(hn_ref[rows, :], wg_ref[...], preferred_element_type=F32) + bg_ref[...]
            t = jnp.dot(br_ref[rows, :].astype(BF16), wb_ref[...], preferred_element_type=F32)
            term = jax.nn.sigmoid(g) * t
            if not first:
                term = acc_ref[rows, :] + term
            if last:
                o_ref[rows, :] = term.astype(o_ref.dtype)
            else:
                acc_ref[rows, :] = term
            return carry

        lax.fori_loop(0, n_chunks, chunk, 0)

    for k, br_ref in enumerate((b0_ref, b1_ref, b2_ref, b3_ref)):
        pl.when(bi == k)(functools.partial(run, br_ref, k == 0, k == N_BRANCH - 1))


def _merge(l, hn, wg, bg, branches, wb):
    M, D = hn.shape
    wide = any(b.dtype == F32 for b in branches)
    tm = _pick(M, 512 if wide else 1024, 8)
    tn = _pick(D, 512, LANES)
    cm = _pick(tm, ROW_CHUNK, 8)
    nj = D // tn
    bw = branches[0].shape[1]
    br_spec = pl.BlockSpec((tm, bw), lambda i, j, b, l: (i, 0))
    return pl.pallas_call(
        functools.partial(_merge_kernel, cm=cm),
        grid_spec=pltpu.PrefetchScalarGridSpec(
            num_scalar_prefetch=1, grid=(M // tm, nj, N_BRANCH),
            in_specs=[pl.BlockSpec((tm, D), lambda i, j, b, l: (i, 0)),
                      pl.BlockSpec((None, D, tn), lambda i, j, b, l: (l[0], 0, b * nj + j)),
                      pl.BlockSpec((None, 1, tn), lambda i, j, b, l: (l[0], 0, b * nj + j)),
                      br_spec, br_spec, br_spec, br_spec,
                      pl.BlockSpec((None, None, bw, tn), lambda i, j, b, l: (l[0], b, 0, j))],
            out_specs=pl.BlockSpec((tm, tn), lambda i, j, b, l: (i, j)),
            scratch_shapes=[pltpu.VMEM((tm, tn), F32)]),
        out_shape=jax.ShapeDtypeStruct((M, D), BF16),
        compiler_params=_cp(("parallel", "arbitrary", "arbitrary")),
        name="merge",
    )(l, hn, wg, bg, *branches, wb)


def _rope128(x, cos, sa, sb):
    return x * cos + pltpu.roll(x, LANES - MLA_ROPE // 2, 1) * sa + pltpu.roll(x, MLA_ROPE // 2, 1) * sb


def _mla_prep_kernel(l_ref, cq_ref, ckv_ref, m1_ref, cos_ref, sa_ref, sb_ref,
                     ncq_ref, wuq_ref, nckv_ref, wkv_ref, qnn_ref, qnr_ref, knn_ref, knr_ref,
                     *outs, sample):
    if sample:
        qatt_ref, qabs_ref, lat_ref, kr_ref = outs
    else:
        qatt_ref, lat_ref, kr_ref, katt_ref, vatt_ref = outs
    cos, sa, sb = cos_ref[...], sa_ref[...], sb_ref[...]
    cq = cq_ref[...]
    cqn = cq * lax.rsqrt(jnp.mean(cq * cq, axis=-1, keepdims=True) + NORM_EPS) * ncq_ref[...]
    q = jnp.dot(cqn.astype(BF16), wuq_ref[...], preferred_element_type=F32)
    for h in range(MLA_HEADS):
        qn = q[:, h * MLA_QW:h * MLA_QW + MLA_NOPE]
        qn = qn * lax.rsqrt(jnp.mean(qn * qn, axis=-1, keepdims=True) + NORM_EPS) * qnn_ref[...]
        qr = q[:, h * MLA_QW + MLA_NOPE:(h + 1) * MLA_QW]
        qr = qr * lax.rsqrt(jnp.sum(qr * qr, axis=-1, keepdims=True) / MLA_ROPE + NORM_EPS) * qnr_ref[...]
        qr = _rope128(qr, cos, sa, sb) * MLA_SCALE
        if sample:
            qg = (qn * knn_ref[...] * MLA_SCALE).astype(BF16)
            qatt_ref[:, h * MLA_QW:h * MLA_QW + MLA_NOPE] = qg.astype(F32)
            qabs_ref[:, h * MLA_QW:(h + 1) * MLA_QW] = jnp.dot(
                qg, wkv_ref[h * MLA_NOPE:(h + 1) * MLA_NOPE, :], preferred_element_type=F32)
        else:
            qatt_ref[:, h * MLA_QW:h * MLA_QW + MLA_NOPE] = (qn * MLA_SCALE).astype(qatt_ref.dtype)
        qatt_ref[:, h * MLA_QW + MLA_NOPE:(h + 1) * MLA_QW] = qr.astype(qatt_ref.dtype)

    ckv = ckv_ref[...]
    lat = ckv * lax.rsqrt(jnp.mean(ckv * ckv, axis=-1, keepdims=True) + NORM_EPS) * nckv_ref[...]
    lat_ref[...] = lat
    lane = lax.broadcasted_iota(jnp.int32, m1_ref.shape, 1)
    kr = jnp.where(lane < MLA_ROPE, m1_ref[...], 0.0)
    kr = kr * lax.rsqrt(jnp.sum(kr * kr, axis=-1, keepdims=True) / MLA_ROPE + NORM_EPS) * knr_ref[...]
    kr = _rope128(kr, cos, sa, sb)
    kr_ref[...] = kr
    if not sample:
        kv = jnp.dot(lat.astype(BF16), wkv_ref[...], preferred_element_type=F32)
        w = MLA_NOPE + MLA_V
        for h in range(MLA_HEADS):
            kn = kv[:, h * w:h * w + MLA_NOPE]
            kn = kn * lax.rsqrt(jnp.mean(kn * kn, axis=-1, keepdims=True) + NORM_EPS) * knn_ref[...]
            katt_ref[:, h * MLA_QW:h * MLA_QW + MLA_NOPE] = kn.astype(BF16)
            katt_ref[:, h * MLA_QW + MLA_NOPE:(h + 1) * MLA_QW] = kr.astype(BF16)
            vatt_ref[:, h * MLA_V:(h + 1) * MLA_V] = kv[:, h * w + MLA_NOPE:(h + 1) * w].astype(BF16)


def _mla_prep(l, z, tabs, W, *, sample):
    M = z.shape[0]
    tm = _pick(min(M, tabs[0].shape[0]), 256, 8)
    nt = tabs[0].shape[0] // tm
    HQ = MLA_HEADS * MLA_QW
    row = lambda w, c: pl.BlockSpec((tm, w), lambda i, l: (i, c))
    tab = pl.BlockSpec((tm, LANES), lambda i, l: (i % nt, 0))
    par = lambda a: pl.BlockSpec((None,) + a.shape[1:], lambda i, l: (l[0],) + (0,) * (a.ndim - 1))
    wkv = W['wk_t'] if sample else W['w_ukv']
    params = [W['norm_cq'], W['w_uq'], W['norm_ckv'], wkv, W['qn_nope'], W['qn_rope'], W['kn_nope'], W['kn_rope']]
    if sample:
        out_shape = [jax.ShapeDtypeStruct((M, HQ), F32), jax.ShapeDtypeStruct((M, HQ), F32)]
        out_specs = [row(HQ, 0), row(HQ, 0)]
    else:
        out_shape = [jax.ShapeDtypeStruct((M, HQ), BF16)]
        out_specs = [row(HQ, 0)]
    out_shape += [jax.ShapeDtypeStruct((M, KV_LORA), F32), jax.ShapeDtypeStruct((M, LANES), F32)]
    out_specs += [row(KV_LORA, 0), row(LANES, 0)]
    if not sample:
        out_shape += [jax.ShapeDtypeStruct((M, HQ), BF16), jax.ShapeDtypeStruct((M, MLA_HEADS * MLA_V), BF16)]
        out_specs += [row(HQ, 0), row(MLA_HEADS * MLA_V, 0)]
    return pl.pallas_call(
        functools.partial(_mla_prep_kernel, sample=sample),
        grid_spec=pltpu.PrefetchScalarGridSpec(
            num_scalar_prefetch=1, grid=(M // tm,),
            in_specs=[row(Q_LORA, C_CQ // Q_LORA), row(KV_LORA, C_CKV // KV_LORA), row(LANES, C_M1 // LANES),
                      tab, tab, tab] + [par(p) for p in params],
            out_specs=out_specs),
        out_shape=out_shape,
        compiler_params=_cp(("parallel",)),
        name="mla_prep_sample" if sample else "mla_prep_prompt",
    )(l, z, z, z, *tabs, *params)


def _mla_prompt_kernel(q_ref, k_ref, v_ref, o_ref, m_ref, l_ref, acc_ref, *, tq, nh):
    qi = pl.program_id(2)
    m_ref[...] = jnp.full(m_ref.shape, NEG_INF, F32)
    l_ref[...] = jnp.zeros(l_ref.shape, F32)
    acc_ref[...] = jnp.zeros(acc_ref.shape, F32)
    row = qi * tq + lax.broadcasted_iota(jnp.int32, (tq, tq), 0)
    col0 = lax.broadcasted_iota(jnp.int32, (tq, tq), 1)
    hs = range(nh)

    def body(j, carry):
        rows = pl.ds(pl.multiple_of(j * tq, tq), tq)
        keep = col0 + j * tq <= row
        s = [lax.dot_general(q_ref[:, h * MLA_QW:(h + 1) * MLA_QW], k_ref[rows, h * MLA_QW:(h + 1) * MLA_QW], NT,
                             preferred_element_type=F32) for h in hs]
        s = [jnp.where(keep, x, NEG_INF) for x in s]
        m_prev = [m_ref[h] for h in hs]
        m_new = [jnp.maximum(m_prev[h], jnp.max(s[h], axis=-1, keepdims=True)) for h in hs]
        p = [jnp.exp(s[h] - m_new[h]) for h in hs]
        pv = [jnp.dot(p[h].astype(BF16), v_ref[rows, h * MLA_V:(h + 1) * MLA_V], preferred_element_type=F32)
              for h in hs]
        for h in hs:
            alpha = jnp.exp(m_prev[h] - m_new[h])
            l_ref[h] = alpha * l_ref[h] + jnp.sum(p[h], axis=-1, keepdims=True)
            acc_ref[h] = alpha * acc_ref[h] + pv[h]
            m_ref[h] = m_new[h]
        return carry

    lax.fori_loop(0, qi + 1, body, 0)
    for h in hs:
        o_ref[:, h * MLA_V:(h + 1) * MLA_V] = (acc_ref[h] / l_ref[h]).astype(o_ref.dtype)


def _mla_prompt(q_att, k_att, v_att, B, T):
    tq = _pick(T, 256, 16)
    nq = T // tq
    nh = 2
    return pl.pallas_call(
        functools.partial(_mla_prompt_kernel, tq=tq, nh=nh),
        grid=(B, MLA_HEADS // nh, nq),
        in_specs=[pl.BlockSpec((tq, nh * MLA_QW), lambda b, h, i: (b * nq + i, h)),
                  pl.BlockSpec((T, nh * MLA_QW), lambda b, h, i: (b, h)),
                  pl.BlockSpec((T, nh * MLA_V), lambda b, h, i: (b, h))],
        out_specs=pl.BlockSpec((tq, nh * MLA_V), lambda b, h, i: (b * nq + i, h)),
        out_shape=jax.ShapeDtypeStruct((B * T, MLA_HEADS * MLA_V), BF16),
        scratch_shapes=[pltpu.VMEM((nh, tq, 1), F32), pltpu.VMEM((nh, tq, 1), F32), pltpu.VMEM((nh, tq, MLA_V), F32)],
        compiler_params=_cp(("parallel", "parallel", "arbitrary")),
        name="mla_prompt",
    )(q_att, k_att, v_att)


def _fetch_pages(layer, page_of, pools, bufs, sems, P):
    ng = pl.num_programs(1)
    step = pl.program_id(0) * ng + pl.program_id(1)
    last = pl.num_programs(0) * ng - 1
    slot = step % 2

    def copy(k, s, sl, i, page):
        return pltpu.make_async_copy(pools[k].at[layer, page], bufs[k].at[sl, i], sems.at[k, sl])

    def start(s, sl):
        for i in range(P):
            page = page_of(s, i)
            for k in range(len(pools)):
                copy(k, s, sl, i, page).start()

    pl.when(step == 0)(lambda: start(step, slot))
    pl.when(step < last)(lambda: start(step + 1, 1 - slot))
    for i in range(P):
        for k in range(len(pools)):
            copy(k, step, slot, i, 0).wait()
    return slot


def _mla_sample_kernel(pt_ref, l_ref, qabs_ref, qatt_ref, latn_ref, krn_ref, wkt_ref, wv_ref, lat_hbm, kr_hbm,
                       o_ref, qp_s, qr_s, m_s, l_s, acc_s, latb, krb, latbuf, krbuf, sems, *, P, nq):
    g = pl.program_id(1)
    R = MLA_HEADS * nq
    slot = _fetch_pages(l_ref[0], lambda s, i: pt_ref[s * P + i], (lat_hbm, kr_hbm), (latbuf, krbuf), sems, P)

    HN = MLA_HEADS * MLA_NOPE

    @pl.when(g == 0)
    def _():
        qp_s[0:HN, :] = wkt_ref[...]
        qp_s[HN:HN + R, :] = jnp.concatenate(
            [qabs_ref[:, h * MLA_QW:(h + 1) * MLA_QW] for h in range(MLA_HEADS)], axis=0).astype(BF16)
        qr_s[...] = jnp.concatenate(
            [qatt_ref[:, h * MLA_QW + MLA_NOPE:h * MLA_QW + MLA_NOPE + MLA_ROPE] for h in range(MLA_HEADS)],
            axis=0).astype(BF16)
        m_s[...] = jnp.full(m_s.shape, NEG_INF, F32)
        l_s[...] = jnp.zeros(l_s.shape, F32)
        acc_s[...] = jnp.zeros(acc_s.shape, F32)

    def scores(big, s2):
        n = big.shape[1]
        rs = []
        for h in range(MLA_HEADS):
            x = big[h * MLA_NOPE:(h + 1) * MLA_NOPE, :]
            r = lax.rsqrt(jnp.sum(x * x, axis=0, keepdims=True) / MLA_NOPE + NORM_EPS)
            rs.append(jnp.broadcast_to(r, (nq, n)))
        return big[HN:HN + R, :] * jnp.concatenate(rs, axis=0) + s2

    def block(lat, kr, mask):
        n = lat.shape[0]
        sub = min(n, 4 * PAGE_SIZE)
        nsub = n // sub
        expand = lambda j: lax.dot_general(qp_s[...], lat[j * sub:(j + 1) * sub, :], NT, preferred_element_type=F32)
        s2 = lax.dot_general(qr_s[...], kr, NT, preferred_element_type=F32)
        bigs = [expand(0)]
        m_run, l_run, acc = m_s[...], l_s[...], acc_s[...]
        for j in range(nsub):
            if j + 1 < nsub:
                bigs.append(expand(j + 1))
            s = scores(bigs[j], s2[:, j * sub:(j + 1) * sub])
            if mask is not None:
                s = jnp.where(mask, s, NEG_INF)
            m_new = jnp.maximum(m_run, jnp.max(s, axis=-1, keepdims=True))
            alpha = jnp.exp(m_run - m_new)
            p = jnp.exp(s - m_new)
            l_run = alpha * l_run + jnp.sum(p, axis=-1, keepdims=True)
            acc = alpha * acc + jnp.dot(p.astype(BF16), lat[j * sub:(j + 1) * sub, :], preferred_element_type=F32)
            m_run = m_new
        m_s[...] = m_run
        l_s[...] = l_run
        acc_s[...] = acc

    for i in range(P):
        latb[i * PAGE_SIZE:(i + 1) * PAGE_SIZE, :] = latbuf[slot, i].astype(BF16)
        krb[i * PAGE_SIZE:(i + 1) * PAGE_SIZE, :] = krbuf[slot, i].astype(BF16)
    block(latb[...], krb[...], None)

    @pl.when(g == pl.num_programs(1) - 1)
    def _():
        pad = PAGE_SIZE - nq
        lat_n = jnp.concatenate([latn_ref[...], jnp.zeros((pad, KV_LORA), F32)], axis=0).astype(BF16)
        kr_n = jnp.concatenate([krn_ref[:, :MLA_ROPE], jnp.zeros((pad, MLA_ROPE), F32)], axis=0).astype(BF16)
        t = lax.broadcasted_iota(jnp.int32, (R, PAGE_SIZE), 0) % nq
        key = lax.broadcasted_iota(jnp.int32, (R, PAGE_SIZE), 1)
        block(lat_n, kr_n, key <= t)
        a = acc_s[...] / l_s[...]
        for h in range(MLA_HEADS):
            hi, lo = _split_bf16(a[h * nq:(h + 1) * nq, :])
            o_ref[:, h * MLA_V:(h + 1) * MLA_V] = (
                jnp.dot(hi, wv_ref[h], preferred_element_type=F32) + jnp.dot(lo, wv_ref[h], preferred_element_type=F32))


def _mla_sample(l, pt, qabs, qatt, cache_lat, cache_kr, lat_new, kr_new, W, Bs, nq):
    NP = pt.shape[0] // Bs
    P = _pick(NP, 32, 1)
    HQ = MLA_HEADS * MLA_QW
    R = MLA_HEADS * nq

    row = lambda w: pl.BlockSpec((nq, w), lambda b, g, pt, l: (b, 0))
    hbm = pl.BlockSpec(memory_space=pl.ANY)
    return pl.pallas_call(
        functools.partial(_mla_sample_kernel, P=P, nq=nq),
        grid_spec=pltpu.PrefetchScalarGridSpec(
            num_scalar_prefetch=2, grid=(Bs, NP // P),
            in_specs=[row(HQ), row(HQ), row(KV_LORA), row(LANES),
                      pl.BlockSpec((None, MLA_HEADS * MLA_NOPE, KV_LORA), lambda b, g, pt, l: (l[0], 0, 0)),
                      pl.BlockSpec((None, MLA_HEADS, KV_LORA, MLA_V), lambda b, g, pt, l: (l[0], 0, 0, 0)),
                      hbm, hbm],
            out_specs=pl.BlockSpec((nq, MLA_HEADS * MLA_V), lambda b, g, pt, l: (b, 0)),
            scratch_shapes=[pltpu.VMEM((MLA_HEADS * MLA_NOPE + R, KV_LORA), BF16), pltpu.VMEM((R, MLA_ROPE), BF16),
                            pltpu.VMEM((R, 1), F32), pltpu.VMEM((R, 1), F32), pltpu.VMEM((R, KV_LORA), F32),
                            pltpu.VMEM((P * PAGE_SIZE, KV_LORA), BF16), pltpu.VMEM((P * PAGE_SIZE, MLA_ROPE), BF16),
                            pltpu.VMEM((2, P, PAGE_SIZE, KV_LORA), F32), pltpu.VMEM((2, P, PAGE_SIZE, MLA_ROPE), F32),
                            pltpu.SemaphoreType.DMA((2, 2))]),
        out_shape=jax.ShapeDtypeStruct((Bs * nq, MLA_HEADS * MLA_V), F32),
        compiler_params=_cp(("arbitrary", "arbitrary")),
        name="mla_sample",
    )(pt, l, qabs, qatt, lat_new, kr_new, W['wk_t'], W['wv'], cache_lat, cache_kr)


def _sb_head(z, v, later_mat, suffix, mask):
    ls = _log_sigmoid(z)
    lsn = ls - z
    if mask is not None:
        lsn = jnp.where(mask, lsn, 0.0)
    hi, lo = _split_bf16(lsn)
    later = (jnp.dot(hi, later_mat, preferred_element_type=F32)
             + jnp.dot(lo, later_mat, preferred_element_type=F32))
    w = jnp.exp(ls + later + suffix)
    if mask is not None:
        w = jnp.where(mask, w, 0.0)
    o = jnp.dot(w.astype(BF16), v, preferred_element_type=F32)
    return o, suffix + later[:, :1] + lsn[:, :1]


def _later_matrix(n):
    return (lax.broadcasted_iota(jnp.int32, (n, n), 0) > lax.broadcasted_iota(jnp.int32, (n, n), 1)).astype(BF16)


def _sb_prompt_kernel(q_ref, k_ref, v_ref, o_ref, qs, acc, *, tq, nh):
    qi = pl.program_id(1)
    low = lax.broadcasted_iota(jnp.int32, (tq, LANES), 1) < SB_DK
    for p in range(nh // 2):
        x = q_ref[:, p * LANES:(p + 1) * LANES] * SB_SCALE
        qs[(2 * p) * tq:(2 * p + 1) * tq, :] = jnp.where(low, x, 0.0).astype(BF16)
        qs[(2 * p + 1) * tq:(2 * p + 2) * tq, :] = jnp.where(low, 0.0, x).astype(BF16)
    later_mat = _later_matrix(tq)

    def kv_block(j):
        rows = pl.ds(pl.multiple_of(j * tq, tq), tq)
        kb = k_ref[rows, :]
        vb = v_ref[rows, :]
        kk = jnp.where(low, pltpu.roll(kb, SB_DK, 1), kb).astype(BF16)
        vv = jnp.where(low, vb, pltpu.roll(vb, SB_DV, 1)).astype(BF16)
        return kk, vv

    def step(j, suffix, mask):
        kk, vv = kv_block(j)
        z = lax.dot_general(qs[...], kk, NT, preferred_element_type=F32)
        return _sb_head(z, vv, later_mat, suffix, mask)

    row = lax.broadcasted_iota(jnp.int32, (nh * tq, tq), 0) % tq
    strict = lax.broadcasted_iota(jnp.int32, (nh * tq, tq), 1) < row
    o, suffix = step(qi, jnp.zeros((nh * tq, 1), F32), strict)
    acc[...] = o

    def body(it, suffix):
        o, suffix = step(qi - 1 - it, suffix, None)
        acc[...] += o
        return suffix

    lax.fori_loop(0, qi, body, suffix)
    for p in range(nh // 2):
        o_ref[:, p * LANES:(p + 1) * LANES] = jnp.where(
            low, acc[(2 * p) * tq:(2 * p + 1) * tq, :], acc[(2 * p + 1) * tq:(2 * p + 2) * tq, :]).astype(o_ref.dtype)


def _sb_prompt(z, B, T):
    tq = _pick(T, 256, 16)
    nq = T // tq
    nh = 16
    w = nh * SB_DK
    return pl.pallas_call(
        functools.partial(_sb_prompt_kernel, tq=tq, nh=nh),
        grid=(B, nq, SB_HEADS // nh),
        in_specs=[pl.BlockSpec((tq, w), lambda b, i, p: (b * nq + i, C_SBQ // w + p)),
                  pl.BlockSpec((T, LANES), lambda b, i, p: (b, C_M1 // LANES)),
                  pl.BlockSpec((T, LANES), lambda b, i, p: (b, C_M2 // LANES))],
        out_specs=pl.BlockSpec((tq, w), lambda b, i, p: (b * nq + i, p)),
        out_shape=jax.ShapeDtypeStruct((B * T, SB_HEADS * SB_DV), BF16),
        scratch_shapes=[pltpu.VMEM((nh * tq, LANES), BF16), pltpu.VMEM((nh * tq, LANES), F32)],
        compiler_params=_cp(("parallel", "parallel", "arbitrary")),
        name="sb_prompt",
    )(z, z, z)


def _sb_sample_kernel(pt_ref, l_ref, q_ref, kn_ref, vn_ref, k_hbm, v_hbm, o_ref, qs, acc, suf, kb, vb, kbuf, vbuf,
                      sems, *, P, NP, nq):
    g = pl.program_id(1)
    R = SB_HEADS * nq
    later_mat = _later_matrix(PAGE_SIZE)
    ng = NP // P

    def page_of(s, i):
        return pt_ref[lax.div(s, ng) * NP + NP - 1 - (lax.rem(s, ng) * P + i)]

    slot = _fetch_pages(l_ref[0], page_of, (k_hbm, v_hbm), (kbuf, vbuf), sems, P)

    @pl.when(g == 0)
    def _():
        x = q_ref[...] * SB_SCALE
        pieces = []
        for h in range(SB_HEADS):
            blk = x[:, (h // 2) * LANES:(h // 2 + 1) * LANES]
            if h % 2:
                blk = pltpu.roll(blk, SB_DK, 1)
            pieces.append(blk[:, :SB_DK])
        qs[...] = jnp.concatenate(pieces, axis=0).astype(BF16)
        pad = PAGE_SIZE - nq
        k_new = pltpu.roll(kn_ref[...], SB_DK, 1)[:, :SB_DK]
        k_new = jnp.concatenate([k_new, jnp.zeros((pad, SB_DK), F32)], axis=0).astype(BF16)
        v_new = jnp.concatenate([vn_ref[:, :SB_DV], jnp.zeros((pad, SB_DV), F32)], axis=0).astype(BF16)
        t = lax.broadcasted_iota(jnp.int32, (R, PAGE_SIZE), 0) % nq
        key = lax.broadcasted_iota(jnp.int32, (R, PAGE_SIZE), 1)
        z = lax.dot_general(qs[...], k_new, NT, preferred_element_type=F32)
        o, s = _sb_head(z, v_new, later_mat, jnp.zeros((R, 1), F32), key < t)
        acc[...] = o
        suf[...] = s

    half = PAGE_SIZE // 2
    for i in range(P):
        for src, dst, d in ((kbuf, kb, SB_DK), (vbuf, vb, SB_DV)):
            x = src[slot, i]
            dst[i * PAGE_SIZE:i * PAGE_SIZE + half, :] = x[:, :d].astype(BF16)
            dst[i * PAGE_SIZE + half:(i + 1) * PAGE_SIZE, :] = pltpu.roll(x, d, 1)[:, :d].astype(BF16)
    pb = 2 * PAGE_SIZE
    rank = lambda c: (1 - c // PAGE_SIZE) * PAGE_SIZE + 2 * (c % half) + (c % PAGE_SIZE) // half
    later2 = (rank(lax.broadcasted_iota(jnp.int32, (pb, pb), 0))
              > rank(lax.broadcasted_iota(jnp.int32, (pb, pb), 1))).astype(BF16)
    z = lax.dot_general(qs[...], kb[...], NT, preferred_element_type=F32)
    ls = _log_sigmoid(z)
    lsn = ls - z
    hi, lo = _split_bf16(lsn)
    prefix = suf[...]
    pieces = []
    for j in range(P // 2):
        cols = slice(j * pb, (j + 1) * pb)
        later = (jnp.dot(hi[:, cols], later2, preferred_element_type=F32)
                 + jnp.dot(lo[:, cols], later2, preferred_element_type=F32))
        pieces.append(later + prefix)
        oldest = j * pb + PAGE_SIZE
        prefix = prefix + later[:, PAGE_SIZE:PAGE_SIZE + 1] + lsn[:, oldest:oldest + 1]
    w = jnp.exp(ls + jnp.concatenate(pieces, axis=1))
    acc[...] += jnp.dot(w.astype(BF16), vb[...], preferred_element_type=F32)
    suf[...] = prefix

    @pl.when(g == pl.num_programs(1) - 1)
    def _():
        o_ref[...] = acc[...]


def _sb_sample(l, pt, z, cache_k, cache_v, Bs, nq):
    NP = pt.shape[0] // Bs
    P = _pick(NP, 32, 2)
    R = SB_HEADS * nq

    hbm = pl.BlockSpec(memory_space=pl.ANY)
    out = pl.pallas_call(
        functools.partial(_sb_sample_kernel, P=P, NP=NP, nq=nq),
        grid_spec=pltpu.PrefetchScalarGridSpec(
            num_scalar_prefetch=2, grid=(Bs, NP // P),
            in_specs=[pl.BlockSpec((nq, SB_HEADS * SB_DK), lambda b, g, pt, l: (b, C_SBQ // (SB_HEADS * SB_DK))),
                      pl.BlockSpec((nq, LANES), lambda b, g, pt, l: (b, C_M1 // LANES)),
                      pl.BlockSpec((nq, LANES), lambda b, g, pt, l: (b, C_M2 // LANES)),
                      hbm, hbm],
            out_specs=pl.BlockSpec((None, R, SB_DV), lambda b, g, pt, l: (b, 0, 0)),
            scratch_shapes=[pltpu.VMEM((R, SB_DK), BF16), pltpu.VMEM((R, SB_DV), F32), pltpu.VMEM((R, 1), F32),
                            pltpu.VMEM((P * PAGE_SIZE, SB_DK), BF16), pltpu.VMEM((P * PAGE_SIZE, SB_DV), BF16),
                            pltpu.VMEM((2, P, PAGE_SIZE // 2, 2 * SB_DK), F32),
                            pltpu.VMEM((2, P, PAGE_SIZE // 2, 2 * SB_DV), F32),
                            pltpu.SemaphoreType.DMA((2, 2))]),
        out_shape=jax.ShapeDtypeStruct((Bs, R, SB_DV), F32),
        compiler_params=_cp(("arbitrary", "arbitrary")),
        name="sb_sample",
    )(pt, l, z, z, z, cache_k, cache_v)
    return out.reshape(Bs, SB_HEADS, nq, SB_DV).transpose(0, 2, 1, 3).reshape(Bs * nq, SB_HEADS * SB_DV)


def _gla_kernel(l_ref, q_ref, k_ref, v_ref, gl_ref, gr_ref, wg_ref, bg_ref, gn_ref, *rest, C, chained):
    if chained:
        o_ref, sout_ref, S = rest

        @pl.when(pl.program_id(1) == 0)
        def _():
            S[...] = jnp.zeros(S.shape, F32)
    else:
        s0_ref, o_ref, sout_ref = rest
    tril = lax.broadcasted_iota(jnp.int32, (C, C), 0) >= lax.broadcasted_iota(jnp.int32, (C, C), 1)
    csum = tril.astype(BF16)
    ones = jnp.ones((C, GLA_DK), BF16)

    def chunk(c, carry):
        rows = pl.ds(pl.multiple_of(c * C, C), C)
        q_all = q_ref[rows, :] * GLA_SCALE
        k_all = k_ref[rows, :]
        v_all = v_ref[rows, :].astype(BF16)
        gr_all = gr_ref[rows, :]
        pre = jnp.dot(gl_ref[rows, :].astype(BF16), wg_ref[...], preferred_element_type=F32) + bg_ref[...]
        hi_all, lo_all = _split_bf16(_log_sigmoid(pre) / GLA_TAU)
        b_all = (jnp.dot(csum, hi_all, preferred_element_type=F32)
                 + jnp.dot(csum, lo_all, preferred_element_type=F32))
        for h in range(GLA_HEADS):
            kcols = slice(h * GLA_DK, (h + 1) * GLA_DK)
            vcols = slice(h * GLA_DV, (h + 1) * GLA_DV)
            qc, kc, vc, b = q_all[:, kcols], k_all[:, kcols], v_all[:, vcols], b_all[:, kcols]
            hi, lo = hi_all[:, kcols], lo_all[:, kcols]
            b_tot = (lax.dot_general(hi, ones, TN, preferred_element_type=F32)
                     + lax.dot_general(lo, ones, TN, preferred_element_type=F32))
            b_last = b[C - 1:C, :]
            qi = (qc * jnp.exp(b)).astype(BF16)
            att = lax.dot_general(qi, (kc * jnp.exp(-b)).astype(BF16), NT, preferred_element_type=F32)
            att = jnp.where(tril, att, 0.0)
            s_old = S[h] if chained else s0_ref[c, h]
            o = (jnp.dot(att.astype(BF16), vc, preferred_element_type=F32)
                 + jnp.dot(qi, s_old.astype(BF16), preferred_element_type=F32))
            kd = (kc * jnp.exp(b_last - b)).astype(BF16)
            decay = jnp.exp(b_tot)
            decay = jnp.concatenate([decay] * (GLA_DV // GLA_DK), axis=1)
            s_new = decay * s_old + lax.dot_general(kd, vc, TN, preferred_element_type=F32)
            if chained:
                S[h] = s_new
            else:
                sout_ref[c, h] = s_new
            on = o * lax.rsqrt(jnp.mean(o * o, axis=-1, keepdims=True) + NORM_EPS) * gn_ref[...]
            gr = gr_all[:, vcols]
            o_ref[rows, vcols] = (on * (gr * jax.nn.sigmoid(gr))).astype(o_ref.dtype)
        return carry

    lax.fori_loop(0, q_ref.shape[0] // C, chunk, 0)
    if chained:
        sout_ref[...] = S[...]


def _gla(l, z, W, B, T, state):
    C = math.gcd(T, GLA_CHUNK)
    chained = state is None
    HK, HV = GLA_HEADS * GLA_DK, GLA_HEADS * GLA_DV
    par = lambda a: pl.BlockSpec((None,) + a.shape[1:], lambda *g: (g[-1][0],) + (0,) * (a.ndim - 1))
    params = [W['w_gla_gate'], W['b_gla_gate'], W['gla_norm']]
    st_shape = jax.ShapeDtypeStruct((B, GLA_HEADS, GLA_DK, GLA_DV), F32)
    if chained:
        tt = _pick(T, 512, C)
        nt = T // tt
        grid = (B, nt)
        cb = lambda w, c0: pl.BlockSpec((tt, w), lambda b, t, l: (b * nt + t, c0 // w))
        in_specs = [cb(HK, C_GQ), cb(HK, C_GK), cb(HV, C_GV), cb(LANES, C_M2), cb(HV, C_GR)] + [par(p) for p in params]
        args = [z] * 5 + params
        st_spec = pl.BlockSpec((None, GLA_HEADS, GLA_DK, GLA_DV), lambda b, t, l: (b, 0, 0, 0))
        o_spec = pl.BlockSpec((tt, HV), lambda b, t, l: (b * nt + t, 0))
        scratch = [pltpu.VMEM((GLA_HEADS, GLA_DK, GLA_DV), F32)]
        sem = ("parallel", "arbitrary")
    else:
        assert T == C
        nb = _pick(B, 8, 1)
        grid = (B // nb,)
        cb = lambda w, c0: pl.BlockSpec((nb * T, w), lambda i, l: (i, c0 // w))
        st_spec = pl.BlockSpec((nb, GLA_HEADS, GLA_DK, GLA_DV), lambda i, l: (i, 0, 0, 0))
        in_specs = ([cb(HK, C_GQ), cb(HK, C_GK), cb(HV, C_GV), cb(LANES, C_M2), cb(HV, C_GR)] + [par(p) for p in params]
                    + [pl.BlockSpec((None, nb, GLA_HEADS, GLA_DK, GLA_DV), lambda i, l: (l[0], i, 0, 0, 0))])
        args = [z] * 5 + params + [state]
        o_spec = pl.BlockSpec((nb * T, HV), lambda i, l: (i, 0))
        scratch = []
        sem = ("parallel",)
    return pl.pallas_call(
        functools.partial(_gla_kernel, C=C, chained=chained),
        grid_spec=pltpu.PrefetchScalarGridSpec(
            num_scalar_prefetch=1, grid=grid, in_specs=in_specs,
            out_specs=[o_spec, st_spec], scratch_shapes=scratch),
        out_shape=[jax.ShapeDtypeStruct((B * T, HV), BF16 if chained else F32), st_shape],
        compiler_params=_cp(sem),
        name="gla_prompt" if chained else "gla_sample",
    )(l, *args)


def _s5_kernel(l_ref, u_ref, bre_ref, bim_ref, pw_ref, cre_ref, cim_ref, d_ref, wglu_ref, bglu_ref, *rest,
               tt, chained):
    if chained:
        o_ref, sre_ref, sim_ref, xre, xim, car_re, car_im = rest
    else:
        x0re_ref, x0im_ref, o_ref, sre_ref, sim_ref, xre, xim = rest
    slab_u = S5_WIDTH // S5_KT
    slab_x = S5_NSTATE // S5_KT
    u = u_ref[...]
    ub = u.astype(BF16)
    for kt in range(S5_KT):
        us = ub[:, kt * slab_u:(kt + 1) * slab_u]
        xre[:, kt * slab_x:(kt + 1) * slab_x] = jnp.dot(us, bre_ref[kt], preferred_element_type=F32)
        xim[:, kt * slab_x:(kt + 1) * slab_x] = jnp.dot(us, bim_ref[kt], preferred_element_type=F32)

    if chained:
        @pl.when(pl.program_id(1) == 0)
        def _():
            car_re[...] = jnp.zeros(car_re.shape, F32)
            car_im[...] = jnp.zeros(car_im.shape, F32)

    def tile(j, carry):
        rows = pl.ds(pl.multiple_of(j * SCAN_ROWS, SCAN_ROWS), SCAN_ROWS)
        for kt in range(S5_KT):
            cols = slice(kt * slab_x, (kt + 1) * slab_x)
            ar, ai = xre[rows, cols], xim[rows, cols]
            for n, k in enumerate((1, 2, 4)):
                pr, pi = pw_ref[2 * n, :, cols], pw_ref[2 * n + 1, :, cols]
                sr, si = pltpu.roll(ar, k, 0), pltpu.roll(ai, k, 0)
                ar, ai = ar + pr * sr - pi * si, ai + pr * si + pi * sr
            if chained:
                cr, ci = car_re[:, cols], car_im[:, cols]
            else:
                cr = jnp.broadcast_to(x0re_ref[pl.ds(j, 1), cols], (SCAN_ROWS, slab_x))
                ci = jnp.broadcast_to(x0im_ref[pl.ds(j, 1), cols], (SCAN_ROWS, slab_x))
            pr, pi = pw_ref[6, :, cols], pw_ref[7, :, cols]
            ar, ai = ar + pr * cr - pi * ci, ai + pr * ci + pi * cr
            xre[rows, cols] = ar
            xim[rows, cols] = ai
            last_r, last_i = ar[SCAN_ROWS - 1:SCAN_ROWS, :], ai[SCAN_ROWS - 1:SCAN_ROWS, :]
            if chained:
                car_re[:, cols] = jnp.broadcast_to(last_r, (SCAN_ROWS, slab_x))
                car_im[:, cols] = jnp.broadcast_to(last_i, (SCAN_ROWS, slab_x))
            else:
                sre_ref[pl.ds(j, 1), cols] = last_r
                sim_ref[pl.ds(j, 1), cols] = last_i
        return carry

    lax.fori_loop(0, tt // SCAN_ROWS, tile, 0)
    if chained:
        sre_ref[...] = car_re[0:1, :]
        sim_ref[...] = car_im[0:1, :]

    ys = []
    for kt in range(S5_KT):
        cols = slice(kt * slab_x, (kt + 1) * slab_x)
        ys.append(jnp.dot(xre[:, cols].astype(BF16), cre_ref[kt], preferred_element_type=F32)
                  - jnp.dot(xim[:, cols].astype(BF16), cim_ref[kt], preferred_element_type=F32))
    y = jnp.concatenate(ys, axis=1) + d_ref[...] * u
    y = jax.nn.gelu(y)
    gate = jax.nn.sigmoid(jnp.dot(y.astype(BF16), wglu_ref[...], preferred_element_type=F32) + bglu_ref[...])
    o_ref[...] = (y * gate).astype(o_ref.dtype)


def _s5(l, z, W, B, T, state):
    chained = state is None
    M = B * T
    par = lambda a: pl.BlockSpec((None,) + a.shape[1:], lambda *g: (g[-1][0],) + (0,) * (a.ndim - 1))
    params = [W['s5_bre'], W['s5_bim'], W['s5_pw'], W['s5_cre'], W['s5_cim'], W['s5_d'], W['w_s5_glu'], W['b_s5_glu']]
    if chained:
        tt = _pick(T, 256, SCAN_ROWS)
        nt = T // tt
        grid = (B, nt)
        u_spec = pl.BlockSpec((tt, S5_WIDTH), lambda b, t, l: (b * nt + t, C_S5 // S5_WIDTH))
        o_spec = pl.BlockSpec((tt, S5_WIDTH), lambda b, t, l: (b * nt + t, 0))
        st_spec = pl.BlockSpec((None, 1, S5_NSTATE), lambda b, t, l: (b, 0, 0))
        st_shape = jax.ShapeDtypeStruct((B, 1, S5_NSTATE), F32)
        in_specs = [u_spec] + [par(p) for p in params]
        args = [z] + params
        scratch = [pltpu.VMEM((tt, S5_NSTATE), F32), pltpu.VMEM((tt, S5_NSTATE), F32),
                   pltpu.VMEM((SCAN_ROWS, S5_NSTATE), F32), pltpu.VMEM((SCAN_ROWS, S5_NSTATE), F32)]
        sem = ("parallel", "arbitrary")
    else:
        assert T == SCAN_ROWS
        tt = _pick(M, 256, 8 * SCAN_ROWS)
        grid = (M // tt,)
        nb = tt // SCAN_ROWS
        u_spec = pl.BlockSpec((tt, S5_WIDTH), lambda i, l: (i, C_S5 // S5_WIDTH))
        o_spec = pl.BlockSpec((tt, S5_WIDTH), lambda i, l: (i, 0))
        st_spec = pl.BlockSpec((nb, S5_NSTATE), lambda i, l: (i, 0))
        st_shape = jax.ShapeDtypeStruct((B, S5_NSTATE), F32)
        x0_spec = pl.BlockSpec((None, nb, S5_NSTATE), lambda i, l: (l[0], i, 0))
        in_specs = [u_spec] + [par(p) for p in params] + [x0_spec, x0_spec]
        args = [z] + params + list(state)
        scratch = [pltpu.VMEM((tt, S5_NSTATE), F32), pltpu.VMEM((tt, S5_NSTATE), F32)]
        sem = ("parallel",)
    return pl.pallas_call(
        functools.partial(_s5_kernel, tt=tt, chained=chained),
        grid_spec=pltpu.PrefetchScalarGridSpec(
            num_scalar_prefetch=1, grid=grid, in_specs=in_specs,
            out_specs=[o_spec, st_spec, st_spec], scratch_shapes=scratch),
        out_shape=[jax.ShapeDtypeStruct((M, S5_WIDTH), BF16 if chained else F32), st_shape, st_shape],
        compiler_params=_cp(sem),
        name="s5_prompt" if chained else "s5_sample",
    )(l, *args)


def _rope_tables(pos):
    half = MLA_ROPE // 2
    inv = ROPE_THETA ** (-jnp.arange(half, dtype=F32) / half)
    ang = pos.astype(F32)[:, None] * inv[None, :]
    cos, sin = jnp.cos(ang), jnp.sin(ang)
    zero = jnp.zeros_like(cos)
    pad = jnp.zeros((pos.shape[0], LANES - MLA_ROPE), F32)
    return (jnp.concatenate([cos, cos, pad], axis=1),
            jnp.concatenate([-sin, zero, pad], axis=1),
            jnp.concatenate([zero, sin, pad], axis=1))


def _s5_tables(a_re, a_im, log_dt, b_re, b_im):
    L = a_re.shape[0]
    dt = jnp.exp(log_dt)[..., None]
    mag = jnp.exp(a_re * dt)
    ab_re, ab_im = mag * jnp.cos(a_im * dt), mag * jnp.sin(a_im * dt)
    den = a_re * a_re + a_im * a_im
    f_re = ((ab_re - 1.0) * a_re + ab_im * a_im) / den
    f_im = (ab_im * a_re - (ab_re - 1.0) * a_im) / den
    bb_re = f_re[..., None] * b_re - f_im[..., None] * b_im
    bb_im = f_re[..., None] * b_im + f_im[..., None] * b_re
    ar, ai = ab_re.reshape(L, S5_NSTATE), ab_im.reshape(L, S5_NSTATE)
    pows = [(ar, ai)]
    for _ in range(SCAN_ROWS - 1):
        pr, pi = pows[-1]
        pows.append((pr * ar - pi * ai, pr * ai + pi * ar))
    idx = jnp.arange(SCAN_ROWS)[None, :, None]
    tabs = []
    for k in (1, 2, 4):
        pr, pi = pows[k - 1]
        tabs.append(jnp.where(idx >= k, pr[:, None, :], 0.0))
        tabs.append(jnp.where(idx >= k, pi[:, None, :], 0.0))
    tabs.append(jnp.stack([p[0] for p in pows], axis=1))
    tabs.append(jnp.stack([p[1] for p in pows], axis=1))
    return bb_re, bb_im, jnp.stack(tabs, axis=1)


def _block_diag_in(bb):
    L = bb.shape[0]
    gpk = S5_GROUPS // S5_KT
    x = bb.reshape(L, S5_KT, gpk, S5_STATE, S5_GROUP)
    eye = jnp.eye(gpk, dtype=bb.dtype)
    y = jnp.einsum('lkgnc,gh->lkgchn', x, eye)
    return y.reshape(L, S5_KT, gpk * S5_GROUP, gpk * S5_STATE).astype(BF16)


def _block_diag_out(c):
    L = c.shape[0]
    gpk = S5_GROUPS // S5_KT
    x = c.reshape(L, S5_KT, gpk, S5_GROUP, S5_STATE)
    eye = jnp.eye(gpk, dtype=c.dtype)
    y = jnp.einsum('lkgcn,gh->lkgnhc', x, eye)
    return y.reshape(L, S5_KT, gpk * S5_STATE, gpk * S5_GROUP).astype(BF16)


def _prepare_weights(p):
    L = p['w_in'].shape[0]
    D = p['w_in'].shape[1]
    W = {}
    offs, o = [], 0
    for n in (Q_LORA, KV_LORA, MLA_ROPE, SB_HEADS * SB_DK, SB_DK, SB_DV, GLA_HEADS * GLA_DK, GLA_HEADS * GLA_DK,
              GLA_HEADS * GLA_DV, GLA_GATE_RANK, GLA_HEADS * GLA_DV, S5_WIDTH):
        offs.append((o, o + n))
        o += n
    (cq, ckv, kr, sbq, sbk, sbv, gq, gk, gv, glow, gr, s5u) = [p['w_in'][:, :, a:b] for a, b in offs]
    zpad = jnp.zeros((L, D, LANES - SB_DV - GLA_GATE_RANK), F32)
    W['w_in'] = jnp.concatenate([sbq, gv, gr, s5u, gq, cq, ckv, gk, kr, sbk, sbv, glow, zpad], axis=2).astype(BF16)
    wq = p['w_uq'].reshape(L, Q_LORA, MLA_HEADS, MLA_NOPE + MLA_ROPE)
    wq = jnp.concatenate([wq, jnp.zeros((L, Q_LORA, MLA_HEADS, MLA_QW - MLA_NOPE - MLA_ROPE), F32)], axis=3)
    W['w_uq'] = wq.reshape(L, Q_LORA, MLA_HEADS * MLA_QW).astype(BF16)
    wkv = p['w_ukv'].astype(BF16)
    W['w_ukv'] = wkv
    wkv4 = wkv.reshape(L, KV_LORA, MLA_HEADS, MLA_NOPE + MLA_V)
    W['wk_t'] = wkv4[..., :MLA_NOPE].transpose(0, 2, 3, 1).reshape(L, MLA_HEADS * MLA_NOPE, KV_LORA)
    W['wv'] = wkv4[..., MLA_NOPE:].transpose(0, 2, 1, 3)
    row = lambda a: a.reshape(L, 1, a.shape[-1])
    padr = lambda a: jnp.concatenate([a, jnp.zeros((L, LANES - MLA_ROPE), F32)], axis=1)
    for k in ('norm_mix', 'norm_cq', 'norm_ckv', 'qn_nope', 'kn_nope', 'b_gla_gate', 'gla_norm', 's5_d', 'b_s5_glu',
              'b_gates', 'norm_ffn'):
        W[k] = row(p[k])
    W['qn_rope'] = row(padr(p['qn_rope']))
    W['kn_rope'] = row(padr(p['kn_rope']))
    wg = jnp.zeros((L, LANES, GLA_HEADS * GLA_DK), F32)
    W['w_gla_gate'] = wg.at[:, SB_DV:SB_DV + GLA_GATE_RANK, :].set(p['w_gla_gate']).astype(BF16)
    bb_re, bb_im, W['s5_pw'] = _s5_tables(p['s5_a_re'], p['s5_a_im'], p['s5_log_dt'], p['s5_b_re'], p['s5_b_im'])
    W['s5_bre'], W['s5_bim'] = _block_diag_in(bb_re), _block_diag_in(bb_im)
    W['s5_cre'], W['s5_cim'] = _block_diag_out(p['s5_c_re']), _block_diag_out(p['s5_c_im'])
    for k in ('w_s5_glu', 'w_gates', 'w_branch', 'w_out', 'w_ff_gate', 'w_ff_up', 'w_ff_down'):
        W[k] = p[k].astype(BF16)
    return W


def _trunk_layer(l, h, W, tabs, B, T, past):
    sample = past is not None
    hn = _rmsnorm(l, h, W['norm_mix'])
    z = _matmul(l, hn, W['w_in'], tn_target=1280)
    if sample:
        pt, cache_lat, cache_kr, cache_k, cache_v, st_gla, st_re, st_im = past
        q_att, q_abs, lat, kr = _mla_prep(l, z, tabs, W, sample=True)
        out_a = _mla_sample(l, pt, q_abs, q_att, cache_lat, cache_kr, lat, kr, W, B, T)
        out_b = _sb_sample(l, pt, z, cache_k, cache_v, B, T)
        out_c, gla_s = _gla(l, z, W, B, T, st_gla)
        out_d, s5_re, s5_im = _s5(l, z, W, B, T, (st_re, st_im))
    else:
        q_att, lat, kr, k_att, v_att = _mla_prep(l, z, tabs, W, sample=False)
        out_a = _mla_prompt(q_att, k_att, v_att, B, T)
        out_b = _sb_prompt(z, B, T)
        out_c, gla_s = _gla(l, z, W, B, T, None)
        out_d, s5_re, s5_im = _s5(l, z, W, B, T, None)
    merged = _merge(l, hn, W['w_gates'], W['b_gates'], (out_a, out_b, out_c, out_d), W['w_branch'])
    h = _matmul(l, merged, W['w_out'], res=h)
    hn2 = _rmsnorm(l, h, W['norm_ffn'])
    act = _ffn_gate_up(l, hn2, W['w_ff_gate'], W['w_ff_up'])
    h = _matmul(l, act, W['w_ff_down'], res=h, tm_target=512, tn_target=512)
    states = (lat.reshape(B, T, KV_LORA),
              kr[:, :MLA_ROPE].reshape(B, T, MLA_ROPE),
              z[:, C_M1 + MLA_ROPE:C_M1 + MLA_ROPE + SB_DK].reshape(B, T, 1, SB_DK),
              z[:, C_M2:C_M2 + SB_DV].reshape(B, T, 1, SB_DV),
              gla_s,
              s5_re.reshape(B, S5_GROUPS, S5_STATE),
              s5_im.reshape(B, S5_GROUPS, S5_STATE))
    return h, states


def kernel(x_prompt, x_sample, cache_mla_latent, cache_mla_krope, cache_sb_k, cache_sb_v, state_gla, state_s5_re, state_s5_im, page_table, norm_mix, w_in, norm_cq, w_uq, norm_ckv, w_ukv, qn_nope, qn_rope, kn_nope, kn_rope, w_gla_gate, b_gla_gate, gla_norm, s5_a_re, s5_a_im, s5_log_dt, s5_b_re, s5_b_im, s5_c_re, s5_c_im, s5_d, w_s5_glu, b_s5_glu, w_gates, b_gates, w_branch, w_out, norm_ffn, w_ff_gate, w_ff_up, w_ff_down):
    L = w_in.shape[0]
    B, T, D = x_prompt.shape
    Bs, Ts, _ = x_sample.shape
    n_pool, page = cache_mla_latent.shape[1], cache_mla_latent.shape[2]
    past_len = page_table.shape[1] * page
    W = _prepare_weights(dict(
        norm_mix=norm_mix, w_in=w_in, norm_cq=norm_cq, w_uq=w_uq, norm_ckv=norm_ckv, w_ukv=w_ukv,
        qn_nope=qn_nope, qn_rope=qn_rope, kn_nope=kn_nope, kn_rope=kn_rope,
        w_gla_gate=w_gla_gate, b_gla_gate=b_gla_gate, gla_norm=gla_norm,
        s5_a_re=s5_a_re, s5_a_im=s5_a_im, s5_log_dt=s5_log_dt, s5_b_re=s5_b_re, s5_b_im=s5_b_im,
        s5_c_re=s5_c_re, s5_c_im=s5_c_im, s5_d=s5_d, w_s5_glu=w_s5_glu, b_s5_glu=b_s5_glu,
        w_gates=w_gates, b_gates=b_gates, w_branch=w_branch, w_out=w_out, norm_ffn=norm_ffn,
        w_ff_gate=w_ff_gate, w_ff_up=w_ff_up, w_ff_down=w_ff_down))
    tabs_p = _rope_tables(jnp.arange(T))
    reps = _pick(Bs * Ts, 256, 8) // Ts
    tabs_s = tuple(jnp.tile(t, (reps, 1)) for t in _rope_tables(past_len + jnp.arange(Ts)))
    pt = page_table.reshape(-1).astype(jnp.int32)
    cache_k = cache_sb_k.reshape(L, n_pool, page // 2, 2 * SB_DK)
    cache_v = cache_sb_v.reshape(L, n_pool, page // 2, 2 * SB_DV)
    st_re = state_s5_re.reshape(L, Bs, S5_NSTATE)
    st_im = state_s5_im.reshape(L, Bs, S5_NSTATE)
    past = (pt, cache_mla_latent, cache_mla_krope, cache_k, cache_v, state_gla, st_re, st_im)

    def layer(carry, li):
        hp, hs = carry
        l = li.reshape(1)
        hp, st_p = _trunk_layer(l, hp, W, tabs_p, B, T, None)
        hs, st_s = _trunk_layer(l, hs, W, tabs_s, Bs, Ts, past)
        return (hp, hs), (st_p, st_s)

    (hp, hs), (sp, ss) = lax.scan(layer, (x_prompt.reshape(B * T, D), x_sample.reshape(Bs * Ts, D)),
                                  jnp.arange(L, dtype=jnp.int32))
    return (hp.reshape(B, T, D), hs.reshape(Bs, Ts, D)) + tuple(sp) + tuple(ss)
```
